```python
import math
import jax, jax.numpy as jnp
from jax import lax
import numpy as np

D_MODEL = 1024
BATCH = 4
SEQ = 4096
DEPTH = 2
DEC_BATCH = 32
DEC_SEQ = 4
PAST_LEN = 8192
PAGE_SIZE = 128

R_HEADS = 4
R_DK = 128
R_DV = 256
R_CHUNK = 128
R_QK = R_HEADS * R_DK
R_V = R_HEADS * R_DV
M_HEADS = 8
M_DH = 64
M_W = M_HEADS * M_DH
M_BLOCK = 256
M_TOPK = 3
M_QBLOCK = 128
ROPE_THETA = 10000.0
D_FF = -(-8 * D_MODEL // (3 * 256)) * 256
IN_WIDTH = 2 * R_QK + 2 * R_V + 3 * M_W + 2 * D_MODEL
NORM_EPS = 1e-6

kernel_name = "hybrid_retention_moba_decoder_step"


def rms_norm(x, g):
    xf = x.astype(jnp.float32)
    y = xf * lax.rsqrt(jnp.mean(xf * xf, axis=-1, keepdims=True) + NORM_EPS)
    return (y * g.astype(jnp.float32)).astype(x.dtype)


def head_rms(o):
    return o * lax.rsqrt(jnp.mean(o * o, axis=-1, keepdims=True) + NORM_EPS)


def split_proj(proj):
    widths = (R_QK, R_QK, R_V, R_V, M_W, M_W, M_W, D_MODEL, D_MODEL)
    out, start = [], 0
    for w in widths:
        out.append(proj[..., start:start + w])
        start += w
    return out


def retnet_rotate(x, pos):
    angle = 1.0 / (10000.0 ** jnp.linspace(0.0, 1.0, R_DK // 2, dtype=jnp.float32))
    ang = jnp.repeat(pos.astype(jnp.float32)[:, None] * angle[None, :], 2, axis=-1)[:, None, :]
    xf = x.astype(jnp.float32)
    rot = jnp.stack((-xf[..., 1::2], xf[..., ::2]), axis=-1).reshape(xf.shape)
    return (xf * jnp.cos(ang) + rot * jnp.sin(ang)).astype(x.dtype)


def rope(x, pos):
    half = M_DH // 2
    inv = ROPE_THETA ** (-jnp.arange(half, dtype=jnp.float32) / half)
    ang = pos.astype(jnp.float32)[:, None] * inv[None, :]
    cos, sin = jnp.cos(ang)[:, None, :], jnp.sin(ang)[:, None, :]
    xf = x.astype(jnp.float32)
    x1, x2 = xf[..., :half], xf[..., half:]
    return jnp.concatenate([x1 * cos - x2 * sin, x2 * cos + x1 * sin], axis=-1).astype(x.dtype)


def retention_chunk(q, k, v, s_prev, log_gamma):
    L = q.shape[1]
    n = jnp.arange(L, dtype=jnp.float32)
    diff = n[:, None] - n[None, :]
    decay = jnp.where(diff >= 0, jnp.exp(log_gamma[:, None, None] * jnp.maximum(diff, 0.0)), 0.0)
    qf, kf, vf = q.astype(jnp.float32), k.astype(jnp.float32), v.astype(jnp.float32)
    inner = jnp.einsum('bihd,bjhd->bhij', qf, kf) * decay[None]
    o = jnp.einsum('bhij,bjhv->bihv', inner, vf)
    q_decay = jnp.exp(log_gamma[None, :] * (n[:, None] + 1.0))
    o = o + jnp.einsum('bihd,bhdv->bihv', qf * q_decay[None, :, :, None], s_prev)
    k_decay = jnp.exp(log_gamma[None, :] * (L - 1.0 - n[:, None]))
    s_new = (jnp.exp(log_gamma * L)[None, :, None, None] * s_prev
             + jnp.einsum('bjhd,bjhv->bhdv', kf * k_decay[None, :, :, None], vf))
    return o, s_new


def retention_prompt(q, k, v, log_gamma):
    B, S = q.shape[:2]
    nc = S // R_CHUNK

    def to_chunks(t):
        return t.reshape((B, nc, R_CHUNK) + t.shape[2:]).swapaxes(0, 1)

    def step(s, xs):
        qc, kc, vc = xs
        o, s = retention_chunk(qc, kc, vc, s, log_gamma)
        return s, o

    s0 = jnp.zeros((B, R_HEADS, R_DK, R_DV), jnp.float32)
    s_fin, o = lax.scan(step, s0, (to_chunks(q), to_chunks(k), to_chunks(v)))
    return o.swapaxes(0, 1).reshape(B, S, R_HEADS, R_DV), s_fin


def moba_select(q, means, n_full, k_sel):
    s = jnp.einsum('hqd,hnd->hqn', q.astype(jnp.float32), means.astype(jnp.float32))
    cand = jnp.arange(means.shape[1])[None, None, :] < n_full[None, :, None]
    s = jnp.where(cand, s, -jnp.inf)
    _, idx = lax.top_k(s, k_sel)
    valid = idx < n_full[None, :, None]
    return idx, valid


def moba_core(q, kg, vg, valid, k_own, v_own, own_mask):
    qf = q.astype(jnp.float32) * (M_DH ** -0.5)
    s_own = jnp.einsum('hqd,hkd->hqk', qf, k_own.astype(jnp.float32))
    s_own = jnp.where(own_mask[None], s_own, -jnp.inf)
    if kg is None:
        p = jax.nn.softmax(s_own, axis=-1)
        return jnp.einsum('hqk,hkd->hqd', p, v_own.astype(jnp.float32))
    H, Q, ks, blk = kg.shape[:4]
    s_past = jnp.einsum('hqd,hqnkd->hqnk', qf, kg.astype(jnp.float32))
    s_past = jnp.where(valid[..., None], s_past, -jnp.inf).reshape(H, Q, ks * blk)
    p = jax.nn.softmax(jnp.concatenate([s_past, s_own], axis=-1), axis=-1)
    p_past = p[..., :ks * blk].reshape(H, Q, ks, blk)
    p_own = p[..., ks * blk:]
    return (jnp.einsum('hqnk,hqnkd->hqd', p_past, vg.astype(jnp.float32))
            + jnp.einsum('hqk,hkd->hqd', p_own, v_own.astype(jnp.float32)))


def moba_prompt(q, k, v):
    B, S = q.shape[:2]
    nb_pad = -(-S // M_BLOCK)
    pad = nb_pad * M_BLOCK - S
    kp = jnp.pad(k, ((0, 0), (0, pad), (0, 0), (0, 0)))
    vp = jnp.pad(v, ((0, 0), (0, pad), (0, 0), (0, 0)))
    kb = kp.reshape(B, nb_pad, M_BLOCK, M_HEADS, M_DH).transpose(0, 3, 1, 2, 4)
    vb = vp.reshape(B, nb_pad, M_BLOCK, M_HEADS, M_DH).transpose(0, 3, 1, 2, 4)
    n_cand = (S - 1) // M_BLOCK
    k_sel = min(M_TOPK, n_cand)
    means = jnp.mean(kb[:, :, :n_cand].astype(jnp.float32), axis=3) if k_sel > 0 else None
    qh = q.transpose(0, 2, 1, 3)
    nq = S // M_QBLOCK

    def one(n):
        b = n // nq
        q0 = (n % nq) * M_QBLOCK
        q_blk = lax.dynamic_slice(qh, (b, 0, q0, 0), (1, M_HEADS, M_QBLOCK, M_DH))[0]
        kb_b = lax.dynamic_index_in_dim(kb, b, 0, keepdims=False)
        vb_b = lax.dynamic_index_in_dim(vb, b, 0, keepdims=False)
        pos = q0 + jnp.arange(M_QBLOCK)
        own = q0 // M_BLOCK
        k_own = lax.dynamic_index_in_dim(kb_b, own, 1, keepdims=False)
        v_own = lax.dynamic_index_in_dim(vb_b, own, 1, keepdims=False)
        own_mask = (own * M_BLOCK + jnp.arange(M_BLOCK))[None, :] <= pos[:, None]
        if k_sel > 0:
            m_b = lax.dynamic_index_in_dim(means, b, 0, keepdims=False)
            idx, valid = moba_select(q_blk, m_b, pos // M_BLOCK, k_sel)
            kg = jax.vmap(lambda t, i: t[i])(kb_b, idx)
            vg = jax.vmap(lambda t, i: t[i])(vb_b, idx)
        else:
            kg = vg = valid = None
        return moba_core(q_blk, kg, vg, valid, k_own, v_own, own_mask)

    out = lax.map(one, jnp.arange(B * nq))
    out = out.reshape(B, nq, M_HEADS, M_QBLOCK, M_DH).transpose(0, 1, 3, 2, 4).reshape(B, S, M_W)
    return out.astype(q.dtype)


def moba_sample(q, k_new, v_new, k_pages, v_pages, page_table):
    DB, L = q.shape[:2]
    n_pages = page_table.shape[1]
    past_len = n_pages * PAGE_SIZE
    ppb = M_BLOCK // PAGE_SIZE
    n_full = past_len // M_BLOCK
    k_sel = min(M_TOPK, n_full)
    tail_page0 = n_full * ppb
    tail_len = past_len - n_full * M_BLOCK
    pos = past_len + jnp.arange(L)
    own_pos = n_full * M_BLOCK + jnp.arange(tail_len + L)
    own_mask = own_pos[None, :] <= pos[:, None]
    n_full_q = pos // M_BLOCK
    hidx = jnp.arange(M_HEADS)[:, None, None, None]

    def one(q_b, kn_b, vn_b, pt_b):
        qh = q_b.swapaxes(0, 1)
        k_tail = k_pages[pt_b[tail_page0:]].reshape(tail_len, M_HEADS, M_DH)
        v_tail = v_pages[pt_b[tail_page0:]].reshape(tail_len, M_HEADS, M_DH)
        k_own = jnp.concatenate([k_tail.astype(jnp.float32), kn_b.astype(jnp.float32)], 0).swapaxes(0, 1)
        v_own = jnp.concatenate([v_tail.astype(jnp.float32), vn_b.astype(jnp.float32)], 0).swapaxes(0, 1)
        if k_sel > 0:
            k_full = k_pages[pt_b[:n_full * ppb]].reshape(n_full, M_BLOCK, M_HEADS, M_DH)
            means = jnp.mean(k_full.astype(jnp.float32), axis=1).swapaxes(0, 1)
            idx, valid = moba_select(qh, means, n_full_q, k_sel)
            phys = pt_b[idx[..., None] * ppb + jnp.arange(ppb)]
            kg = k_pages[phys, :, hidx].reshape(M_HEADS, L, k_sel, M_BLOCK, M_DH)
            vg = v_pages[phys, :, hidx].reshape(M_HEADS, L, k_sel, M_BLOCK, M_DH)
        else:
            kg = vg = valid = None
        return moba_core(qh, kg, vg, valid, k_own, v_own, own_mask)

    out = jax.vmap(one)(q, k_new, v_new, page_table)
    return out.swapaxes(1, 2).reshape(DB, L, M_W).astype(q.dtype)


def forward(x, pos, ret_mix, attn_mix, g_mix, w_in, w_br_ret, w_br_att, w_out, g_ffn, w_gu, w_down, g_final):
    B, L = x.shape[:2]
    states = []
    for l in range(DEPTH):
        h = rms_norm(x, g_mix[l])
        proj = jnp.einsum('bld,de->ble', h, w_in[l])
        rq, rk, rv, rg, mq, mk, mv, gate_r, gate_a = split_proj(proj)
        rq = retnet_rotate(rq.reshape(B, L, R_HEADS, R_DK), pos)
        rk = retnet_rotate(rk.reshape(B, L, R_HEADS, R_DK), pos) * (R_DK ** -0.5)
        rv = rv.reshape(B, L, R_HEADS, R_DV)
        ro, r_state = ret_mix(l, rq, rk, rv)
        ro = (head_rms(ro).reshape(B, L, R_V) * jax.nn.silu(rg.astype(jnp.float32))).astype(x.dtype)
        mq = rope(mq.reshape(B, L, M_HEADS, M_DH), pos)
        mk = rope(mk.reshape(B, L, M_HEADS, M_DH), pos)
        mv = mv.reshape(B, L, M_HEADS, M_DH)
        mo = attn_mix(l, mq, mk, mv)
        merged = (jax.nn.sigmoid(gate_r) * (ro @ w_br_ret[l])
                  + jax.nn.sigmoid(gate_a) * (mo @ w_br_att[l]))
        x = x + merged @ w_out[l]
        h2 = rms_norm(x, g_ffn[l])
        gu = h2 @ w_gu[l]
        x = x + (jax.nn.silu(gu[..., :D_FF]) * gu[..., D_FF:]) @ w_down[l]
        states.append((mk, mv, r_state))
    return rms_norm(x, g_final), states


def setup_inputs(seed: int = 0) -> dict:
    key = jax.random.key(seed)
    ks = jax.random.split(key, 18)
    f32 = jnp.float32
    n_pages = PAST_LEN // PAGE_SIZE
    n_pool = (DEC_BATCH * n_pages * 5) // 4

    def nrm(k, shape, scale):
        return jax.random.normal(k, shape, f32) * scale

    perm = jax.random.permutation(ks[5], n_pool)[: DEC_BATCH * n_pages]
    return {
        "x_prompt": nrm(ks[0], (BATCH, SEQ, D_MODEL), 1.0),
        "x_sample": nrm(ks[1], (DEC_BATCH, DEC_SEQ, D_MODEL), 1.0),
        "cache_k": nrm(ks[2], (DEPTH, n_pool, PAGE_SIZE, M_HEADS, M_DH), 1.0),
        "cache_v": nrm(ks[3], (DEPTH, n_pool, PAGE_SIZE, M_HEADS, M_DH), 1.0),
        "state_ret": nrm(ks[4], (DEPTH, DEC_BATCH, R_HEADS, R_DK, R_DV), 1.0),
        "page_table": perm.reshape(DEC_BATCH, n_pages).astype(jnp.int32),
        "g_mix": 1.0 + nrm(ks[6], (DEPTH, D_MODEL), 0.02),
        "w_in": nrm(ks[7], (DEPTH, D_MODEL, IN_WIDTH), D_MODEL ** -0.5),
        "w_br_ret": nrm(ks[8], (DEPTH, R_V, D_MODEL), R_V ** -0.5),
        "w_br_att": nrm(ks[9], (DEPTH, M_W, D_MODEL), M_W ** -0.5),
        "w_out": nrm(ks[10], (DEPTH, D_MODEL, D_MODEL), D_MODEL ** -0.5),
        "g_ffn": 1.0 + nrm(ks[11], (DEPTH, D_MODEL), 0.02),
        "w_gu": nrm(ks[12], (DEPTH, D_MODEL, 2 * D_FF), D_MODEL ** -0.5),
        "w_down": nrm(ks[13], (DEPTH, D_FF, D_MODEL), D_FF ** -0.5),
        "g_final": 1.0 + nrm(ks[14], (D_MODEL,), 0.02),
    }


def reference(x_prompt, x_sample, cache_k, cache_v, state_ret, page_table,
              g_mix, w_in, w_br_ret, w_br_att, w_out, g_ffn, w_gu, w_down, g_final):
    weights = (g_mix, w_in, w_br_ret, w_br_att, w_out, g_ffn, w_gu, w_down, g_final)
    log_gamma = jnp.log1p(-jnp.exp2(-5.0 - jnp.arange(R_HEADS, dtype=jnp.float32)))

    pos_p = jnp.arange(x_prompt.shape[1])

    def ret_prompt(l, q, k, v):
        return retention_prompt(q, k, v, log_gamma)

    def att_prompt(l, q, k, v):
        return moba_prompt(q, k, v)

    y_prompt, st_p = forward(x_prompt, pos_p, ret_prompt, att_prompt, *weights)

    past_len = page_table.shape[1] * PAGE_SIZE
    pos_s = past_len + jnp.arange(x_sample.shape[1])

    def ret_sample(l, q, k, v):
        return retention_chunk(q, k, v, state_ret[l].astype(jnp.float32), log_gamma)

    def att_sample(l, q, k, v):
        return moba_sample(q, k, v, cache_k[l], cache_v[l], page_table)

    y_sample, st_s = forward(x_sample, pos_s, ret_sample, att_sample, *weights)

    k_prompt = jnp.stack([s[0] for s in st_p])
    v_prompt = jnp.stack([s[1] for s in st_p])
    state_ret_prompt = jnp.stack([s[2] for s in st_p]).astype(state_ret.dtype)
    k_sample = jnp.stack([s[0] for s in st_s])
    v_sample = jnp.stack([s[1] for s in st_s])
    state_ret_sample = jnp.stack([s[2] for s in st_s]).astype(state_ret.dtype)
    return (y_prompt, y_sample, k_prompt, v_prompt, state_ret_prompt, k_sample, v_sample, state_ret_sample)
```

```python
import functools

import jax
import jax.numpy as jnp
from jax import lax
from jax.experimental import pallas as pl
from jax.experimental.pallas import tpu as pltpu

F32 = jnp.float32
BF16 = jnp.bfloat16

D_MODEL = 1024
R_HEADS = 4
R_DK = 128
R_DV = 256
R_CHUNK = 128
R_QK = R_HEADS * R_DK
R_V = R_HEADS * R_DV
M_HEADS = 8
M_DH = 64
M_W = M_HEADS * M_DH
M_BLOCK = 256
M_TOPK = 3
PAGE_SIZE = 128
ROPE_THETA = 10000.0
D_FF = 2816
NORM_EPS = 1e-6
NEG = -1e30

C_RQ, C_RK, C_RV, C_RG = 0, 512, 1024, 2048
C_MQ, C_MV_END = 3072, 4608
C_GR, C_GA = 4608, 5632
IN_WIDTH = 6656

LANES = 128
VMEM_LIMIT = 56 * 1024 * 1024
HIGHEST = lax.Precision.HIGHEST

NT_DIMS = (((1,), (1,)), ((), ()))
TN_DIMS = (((0,), (0,)), ((), ()))


def _params(n_axes):
    return pltpu.CompilerParams(
        dimension_semantics=("arbitrary",) * n_axes, vmem_limit_bytes=VMEM_LIMIT)


def _dot(a, b, precision=None):
    return jnp.dot(a, b, precision=precision, preferred_element_type=F32)


def _dot_nt(a, b):
    return lax.dot_general(a, b, NT_DIMS, preferred_element_type=F32)


def _dot_tn(a, b):
    return lax.dot_general(a, b, TN_DIMS, preferred_element_type=F32)


def _resident(shape):
    nd = len(shape)
    return pl.BlockSpec(shape, lambda *_: (0,) * nd, pipeline_mode=pl.Buffered(1))


def _rms(x, g):
    return x * lax.rsqrt(jnp.mean(x * x, axis=-1, keepdims=True) + NORM_EPS) * g


def _in_proj_kernel(x_ref, g_ref, w_ref, wmt_ref, cr_ref, sr_ref, cmt_ref, smt_ref,
                    rq_ref, rk_ref, rv_ref, rg_ref, gr_ref, ga_ref, mqt_ref, mkt_ref, mvt_ref,
                    h_scr):
    h_scr[...] = _rms(x_ref[...], g_ref[...]).astype(BF16)

    def seg(c0, width=512):
        return _dot(h_scr[...], w_ref[:, c0:c0 + width])

    tm = h_scr.shape[0]
    even_lane = (lax.broadcasted_iota(jnp.int32, (tm, LANES), 1) % 2) == 0
    cr, sr = cr_ref[...], sr_ref[...]

    for out_ref, c0, scale in ((rq_ref, C_RQ, None), (rk_ref, C_RK, R_DK ** -0.5)):
        a = seg(c0)
        for hh in range(R_HEADS):
            sl = slice(hh * LANES, (hh + 1) * LANES)
            ah = a[:, sl]
            partner = jnp.where(even_lane, pltpu.roll(ah, LANES - 1, 1), pltpu.roll(ah, 1, 1))
            r = ah * cr + partner * sr
            if scale is not None:
                r = r * scale
            out_ref[:, sl] = r.astype(out_ref.dtype)
    for j in range(2):
        cols = slice(j * 512, (j + 1) * 512)
        rv_ref[:, cols] = seg(C_RV + j * 512).astype(rv_ref.dtype)
        rg_ref[:, cols] = seg(C_RG + j * 512).astype(rg_ref.dtype)
        gr_ref[:, cols] = seg(C_GR + j * 512).astype(gr_ref.dtype)
        ga_ref[:, cols] = seg(C_GA + j * 512).astype(ga_ref.dtype)

    cmt, smt = cmt_ref[...], smt_ref[...]
    half = M_DH // 2
    for idx, out_ref in enumerate((mqt_ref, mkt_ref, mvt_ref)):
        at = _dot_nt(wmt_ref[idx * M_W:(idx + 1) * M_W, :], h_scr[...])
        if idx == 2:
            out_ref[...] = at
            continue
        for hd in range(M_HEADS):
            x1 = at[hd * M_DH:hd * M_DH + half]
            x2 = at[hd * M_DH + half:(hd + 1) * M_DH]
            out_ref[hd * M_DH:hd * M_DH + half, :] = x1 * cmt - x2 * smt
            out_ref[hd * M_DH + half:(hd + 1) * M_DH, :] = x2 * cmt + x1 * smt


def _in_proj(x, g, w, wmt, tabs, n_seq, tm, r_dtype):
    T = x.shape[0]
    S = T // n_seq
    per_seq = S // tm
    cr, sr, cmt, smt = tabs
    row = lambda i: (i, 0)
    tab_spec = pl.BlockSpec((tm, LANES), lambda i: (i % per_seq, 0))
    tabt_spec = pl.BlockSpec((M_DH // 2, tm), lambda i: (0, i % per_seq))
    widths = (R_QK, R_QK, R_V, R_V, D_MODEL, D_MODEL)
    dtypes = (r_dtype, r_dtype, r_dtype, F32, F32, F32)
    featmajor = pl.BlockSpec((None, M_W, tm), lambda i: (i // per_seq, 0, i % per_seq))
    return pl.pallas_call(
        _in_proj_kernel,
        grid=(T // tm,),
        in_specs=[pl.BlockSpec((tm, D_MODEL), row), _resident((1, D_MODEL)),
                  _resident(w.shape), _resident(wmt.shape),
                  tab_spec, tab_spec, tabt_spec, tabt_spec],
        out_specs=[pl.BlockSpec((tm, wd), row) for wd in widths] + [featmajor] * 3,
        out_shape=[jax.ShapeDtypeStruct((T, wd), dt) for wd, dt in zip(widths, dtypes)]
        + [jax.ShapeDtypeStruct((n_seq, M_W, S), F32)] * 3,
        scratch_shapes=[pltpu.VMEM((tm, D_MODEL), BF16)],
        compiler_params=_params(1),
        name="in_proj",
    )(x, g, w, wmt, cr, sr, cmt, smt)


def _head_rms_gate(o, gate):
    on = o * lax.rsqrt(jnp.mean(o * o, axis=-1, keepdims=True) + NORM_EPS)
    return on * (gate * jax.nn.sigmoid(gate))


def _retention_head(q, k, v, s_prev, dec, qd, kd, gl):
    qf, kf = q.astype(F32), k.astype(F32)
    vb = v.astype(BF16)
    inner = _dot_nt(q.astype(BF16), k.astype(BF16)) * dec
    o = _dot(inner.astype(BF16), vb) + _dot((qf * qd).astype(BF16), s_prev.astype(BF16))
    s_new = gl * s_prev + _dot_tn((kf * kd).astype(BF16), vb)
    return o, s_new


def _ret_prompt_kernel(q_ref, k_ref, v_ref, g_ref, dec_ref, qd_ref, kd_ref, gl_ref,
                       o_ref, st_ref, s_scr):
    c = pl.program_id(1)

    @pl.when(c == 0)
    def _():
        s_scr[...] = jnp.zeros_like(s_scr)

    for hh in range(R_HEADS):
        ks = slice(hh * R_DK, (hh + 1) * R_DK)
        vs = slice(hh * R_DV, (hh + 1) * R_DV)
        o, s_new = _retention_head(q_ref[:, ks], k_ref[:, ks], v_ref[:, vs], s_scr[hh],
                                   dec_ref[hh], qd_ref[hh], kd_ref[hh], gl_ref[hh])
        s_scr[hh] = s_new
        o_ref[:, vs] = _head_rms_gate(o, g_ref[:, vs]).astype(o_ref.dtype)

    @pl.when(c == pl.num_programs(1) - 1)
    def _():
        st_ref[...] = s_scr[...]


def _retention_prompt(rq, rk, rv, rg, tabs, B, S):
    nc = S // R_CHUNK
    row = lambda b, c: (b * nc + c, 0)
    dec, qd, kd, gl = tabs
    return pl.pallas_call(
        _ret_prompt_kernel,
        grid=(B, nc),
        in_specs=[pl.BlockSpec((R_CHUNK, R_QK), row), pl.BlockSpec((R_CHUNK, R_QK), row),
                  pl.BlockSpec((R_CHUNK, R_V), row), pl.BlockSpec((R_CHUNK, R_V), row),
                  _resident(dec.shape), _resident(qd.shape), _resident(kd.shape),
                  _resident(gl.shape)],
        out_specs=[pl.BlockSpec((R_CHUNK, R_V), row),
                   pl.BlockSpec((None, R_HEADS, R_DK, R_DV), lambda b, c: (b, 0, 0, 0))],
        out_shape=[jax.ShapeDtypeStruct((B * S, R_V), BF16),
                   jax.ShapeDtypeStruct((B, R_HEADS, R_DK, R_DV), F32)],
        scratch_shapes=[pltpu.VMEM((R_HEADS, R_DK, R_DV), F32)],
        compiler_params=_params(2),
        name="retention_prompt",
    )(rq, rk, rv, rg, dec, qd, kd, gl)


def _ret_sample_kernel(q_ref, k_ref, v_ref, g_ref, s_ref, dec_ref, qd_ref, kd_ref, gl_ref,
                       o_ref, st_ref):
    for hh in range(R_HEADS):
        ks = slice(hh * R_DK, (hh + 1) * R_DK)
        vs = slice(hh * R_DV, (hh + 1) * R_DV)
        o, s_new = _retention_head(q_ref[:, ks], k_ref[:, ks], v_ref[:, vs], s_ref[hh],
                                   dec_ref[hh], qd_ref[hh], kd_ref[hh], gl_ref[hh])
        st_ref[hh] = s_new
        o_ref[:, vs] = _head_rms_gate(o, g_ref[:, vs]).astype(o_ref.dtype)


def _retention_sample(rq, rk, rv, rg, state_all, layer, tabs):
    DB, LP = rq.shape[:2]
    dec, qd, kd, gl = tabs
    tok = lambda w: pl.BlockSpec((None, LP, w), lambda b: (b, 0, 0))
    st_in = pl.BlockSpec((None, None, R_HEADS, R_DK, R_DV), lambda b: (layer, b, 0, 0, 0))
    return pl.pallas_call(
        _ret_sample_kernel,
        grid=(DB,),
        in_specs=[tok(R_QK), tok(R_QK), tok(R_V), tok(R_V), st_in,
                  _resident(dec.shape), _resident(qd.shape), _resident(kd.shape),
                  _resident(gl.shape)],
        out_specs=[tok(R_V),
                   pl.BlockSpec((None, R_HEADS, R_DK, R_DV), lambda b: (b, 0, 0, 0))],
        out_shape=[jax.ShapeDtypeStruct((DB, LP, R_V), F32),
                   jax.ShapeDtypeStruct((DB, R_HEADS, R_DK, R_DV), F32)],
        compiler_params=_params(1),
        name="retention_sample",
    )(rq, rk, rv, rg, state_all, dec, qd, kd, gl)


def _retention_tables(L, l_real):
    log_gamma = jnp.log1p(-jnp.exp2(-5.0 - jnp.arange(R_HEADS, dtype=F32)))
    n = jnp.arange(L, dtype=F32)
    diff = n[:, None] - n[None, :]
    dec = jnp.where(diff >= 0, jnp.exp(log_gamma[:, None, None] * jnp.maximum(diff, 0.0)), 0.0)
    q_decay = jnp.exp(log_gamma[:, None] * (n[None, :] + 1.0))
    k_decay = jnp.exp(log_gamma[:, None] * (l_real - 1.0 - n[None, :]))
    qd = jnp.broadcast_to(q_decay[:, :, None], (R_HEADS, L, R_DK))
    kd = jnp.broadcast_to(k_decay[:, :, None], (R_HEADS, L, R_DK))
    gl = jnp.broadcast_to(jnp.exp(log_gamma * l_real)[:, None, None], (R_HEADS, 1, R_DV))
    return dec, qd, kd, gl


def _rank_select(scores, n_idx, n_cand, axis):
    nb = scores.shape[axis]
    cnt = jnp.zeros(scores.shape, F32)
    for m in range(nb):
        sm = scores[m:m + 1, :] if axis == 0 else scores[:, m:m + 1]
        beats = (sm > scores) | ((sm == scores) & (m < n_idx))
        cnt = cnt + jnp.where(beats, jnp.where(m < n_cand, 1.0, 0.0), 0.0)
    return jnp.where((n_idx < n_cand) & (cnt < float(M_TOPK)), 1.0, 0.0)


def _moba_prompt_kernel(qt_ref, kt_ref, vt_ref, o_ref, kb_scr, vb_scr, means_scr, sel_scr):
    i = pl.program_id(2)
    nb = kb_scr.shape[0]

    @pl.when(i == 0)
    def _():
        for n in range(nb):
            cols = slice(n * M_BLOCK, (n + 1) * M_BLOCK)
            kblk = kt_ref[:, cols].T
            kb_scr[n] = kblk.astype(BF16)
            means_scr[n:n + 1, :] = jnp.sum(kblk, axis=0, keepdims=True) * (1.0 / M_BLOCK)
            vb_scr[n] = vt_ref[:, cols].astype(BF16)

    qt = qt_ref[...]
    tq = qt.shape[1]
    row = lax.broadcasted_iota(jnp.int32, qt.shape, 0)
    n_idx = lax.broadcasted_iota(jnp.int32, (nb, tq), 0)
    key_idx = lax.broadcasted_iota(jnp.int32, (M_BLOCK, tq), 0)
    qry_idx = lax.broadcasted_iota(jnp.int32, (M_BLOCK, tq), 1)
    means = means_scr[...]
    heads = []
    for hd in range(2):
        mine = (row < M_DH) if hd == 0 else (row >= M_DH)
        qh = jnp.where(mine, qt, 0.0)
        sel_scr[hd] = _rank_select(_dot(means, qh, precision=HIGHEST), n_idx, i, axis=0)
        heads.append((qh * (M_DH ** -0.5)).astype(BF16))

    def scores_t(n, qb):
        return _dot(kb_scr[n], qb)

    def v_t(n, hd):
        return vb_scr[n, hd * M_DH:(hd + 1) * M_DH, :]

    carry = []
    for hd in range(2):
        s = jnp.where(key_idx <= qry_idx, scores_t(i, heads[hd]), NEG)
        m = jnp.max(s, axis=0, keepdims=True)
        p = jnp.exp(s - m)
        carry += [m, jnp.sum(p, axis=0, keepdims=True), _dot(v_t(i, hd), p.astype(BF16))]

    def body(n, carry):
        out = []
        for hd in range(2):
            m, l, acc = carry[3 * hd:3 * hd + 3]
            keep = sel_scr[hd, pl.ds(n, 1), :] > 0.5
            s = jnp.where(keep, scores_t(n, heads[hd]), NEG)
            m_new = jnp.maximum(m, jnp.max(s, axis=0, keepdims=True))
            a = jnp.exp(m - m_new)
            p = jnp.exp(s - m_new)
            out += [m_new, a * l + jnp.sum(p, axis=0, keepdims=True),
                    a * acc + _dot(v_t(n, hd), p.astype(BF16))]
        return tuple(out)

    carry = lax.fori_loop(0, i, body, tuple(carry))
    o_t = jnp.concatenate([carry[2] / carry[1], carry[5] / carry[4]], axis=0)
    o_ref[...] = o_t.T.astype(o_ref.dtype)


def _moba_prompt(mqt, mkt, mvt):
    B, _, S = mqt.shape
    nb = S // M_BLOCK
    tq = M_BLOCK
    npair = M_W // LANES
    return pl.pallas_call(
        _moba_prompt_kernel,
        grid=(B, npair, nb),
        in_specs=[pl.BlockSpec((None, LANES, tq), lambda b, p, i: (b, p, i)),
                  pl.BlockSpec((None, LANES, S), lambda b, p, i: (b, p, 0)),
                  pl.BlockSpec((None, LANES, S), lambda b, p, i: (b, p, 0))],
        out_specs=pl.BlockSpec((tq, LANES), lambda b, p, i: (b * nb + i, p)),
        out_shape=jax.ShapeDtypeStruct((B * S, M_W), BF16),
        scratch_shapes=[pltpu.VMEM((nb, M_BLOCK, LANES), BF16),
                        pltpu.VMEM((nb, LANES, M_BLOCK), BF16),
                        pltpu.VMEM((nb, LANES), F32),
                        pltpu.VMEM((2, nb, tq), F32)],
        compiler_params=_params(3),
        name="moba_prompt",
    )(mqt, mkt, mvt)


PAGES_PER_STEP = 16


def _own_head(shape):
    row = lax.broadcasted_iota(jnp.int32, shape, 0)
    lane = lax.broadcasted_iota(jnp.int32, shape, 1)
    return (lane // M_DH) == (row % M_HEADS)


def _moba_keys_kernel(pt_ref, q_ref, *refs):
    k_refs = refs[:PAGES_PER_STEP]
    s_ref, bsum_ref = refs[PAGES_PER_STEP:]
    step = pl.program_id(1)

    @pl.when(step == 0)
    def _():
        bsum_ref[...] = jnp.zeros_like(bsum_ref)

    q = q_ref[...]
    qb = (jnp.where(_own_head(q.shape), q, 0.0) * (M_DH ** -0.5)).astype(BF16)
    ppb = M_BLOCK // PAGE_SIZE
    blk_lane = lax.broadcasted_iota(jnp.int32, bsum_ref.shape, 1)
    bsum = bsum_ref[...]
    for j in range(PAGES_PER_STEP):
        kpt = k_refs[j][...].reshape(M_W, PAGE_SIZE)
        s_ref[:, j * PAGE_SIZE:(j + 1) * PAGE_SIZE] = _dot(qb, kpt.astype(BF16))
        rowsum = jnp.sum(kpt, axis=1, keepdims=True)
        n = step * (PAGES_PER_STEP // ppb) + j // ppb
        bsum = bsum + jnp.where(blk_lane == n, rowsum, 0.0)
    bsum_ref[...] = bsum


def _page_specs(layer):
    def spec(j):
        return pl.BlockSpec(
            (None, None, M_HEADS, M_DH, PAGE_SIZE),
            lambda b, s, pt: (layer, pt[b, s * PAGES_PER_STEP + j], 0, 0, 0))
    return [spec(j) for j in range(PAGES_PER_STEP)]


def _moba_keys(page_table, q_rep, cache_kt, layer):
    DB, n_pages = page_table.shape
    R = q_rep.shape[1]
    steps = n_pages // PAGES_PER_STEP
    nb = n_pages * PAGE_SIZE // M_BLOCK
    grid_spec = pltpu.PrefetchScalarGridSpec(
        num_scalar_prefetch=1,
        grid=(DB, steps),
        in_specs=[pl.BlockSpec((None, R, M_W), lambda b, s, pt: (b, 0, 0))] + _page_specs(layer),
        out_specs=[pl.BlockSpec((None, R, PAGES_PER_STEP * PAGE_SIZE), lambda b, s, pt: (b, 0, s)),
                   pl.BlockSpec((None, M_W, nb), lambda b, s, pt: (b, 0, 0))],
    )
    return pl.pallas_call(
        _moba_keys_kernel,
        grid_spec=grid_spec,
        out_shape=[jax.ShapeDtypeStruct((DB, R, n_pages * PAGE_SIZE), F32),
                   jax.ShapeDtypeStruct((DB, M_W, nb), F32)],
        compiler_params=_params(2),
        name="moba_sample_keys",
    )(page_table, q_rep, *([cache_kt] * PAGES_PER_STEP))


def _moba_probs_kernel(s_ref, bsum_ref, q_ref, kn_ref, vn_ref, p_ref, own_ref):
    q = q_ref[...]
    R = q.shape[0]
    qh = jnp.where(_own_head(q.shape), q, 0.0)
    means_t = bsum_ref[...] * (1.0 / M_BLOCK)
    nb = means_t.shape[1]
    sc = _dot(qh, means_t, precision=HIGHEST)
    n_idx = lax.broadcasted_iota(jnp.int32, (R, nb), 1)
    sel = _rank_select(sc, n_idx, nb, axis=1)
    s = s_ref[...]
    past = s.shape[1]
    blk_of_key = lax.broadcasted_iota(jnp.int32, (nb, past), 1) // M_BLOCK
    expand = jnp.where(blk_of_key == lax.broadcasted_iota(jnp.int32, (nb, past), 0), 1.0, 0.0)
    keep = _dot(sel.astype(BF16), expand.astype(BF16)) > 0.5
    s = jnp.where(keep, s, NEG)
    qb = (qh * (M_DH ** -0.5)).astype(BF16)
    s_own = _dot_nt(qb, kn_ref[...].astype(BF16))
    tok = lax.broadcasted_iota(jnp.int32, s_own.shape, 0) // M_HEADS
    s_own = jnp.where(lax.broadcasted_iota(jnp.int32, s_own.shape, 1) <= tok, s_own, NEG)
    m = jnp.maximum(jnp.max(s, axis=1, keepdims=True), jnp.max(s_own, axis=1, keepdims=True))
    p = jnp.exp(s - m)
    p_own = jnp.exp(s_own - m)
    inv = 1.0 / (jnp.sum(p, axis=1, keepdims=True) + jnp.sum(p_own, axis=1, keepdims=True))
    p_ref[...] = (p * inv).astype(p_ref.dtype)
    own_ref[...] = _dot((p_own * inv).astype(BF16), vn_ref[...].astype(BF16))


def _moba_probs(scores, bsum_t, q_rep, k_new, v_new):
    DB, R, past = scores.shape
    nb = bsum_t.shape[2]
    LP = k_new.shape[1]
    per_b = lambda *shape: pl.BlockSpec((None,) + shape, lambda b: (b, 0, 0))
    return pl.pallas_call(
        _moba_probs_kernel,
        grid=(DB,),
        in_specs=[per_b(R, past), per_b(M_W, nb), per_b(R, M_W), per_b(LP, M_W), per_b(LP, M_W)],
        out_specs=[per_b(R, past), per_b(R, M_W)],
        out_shape=[jax.ShapeDtypeStruct((DB, R, past), BF16),
                   jax.ShapeDtypeStruct((DB, R, M_W), F32)],
        compiler_params=_params(1),
        name="moba_sample_probs",
    )(scores, bsum_t, q_rep, k_new, v_new)


def _moba_values_kernel(pt_ref, p_ref, own_ref, *refs):
    v_refs = refs[:PAGES_PER_STEP]
    o_ref, acc_scr = refs[PAGES_PER_STEP:]
    step = pl.program_id(1)

    @pl.when(step == 0)
    def _():
        acc_scr[...] = own_ref[...]

    acc = acc_scr[...]
    for j in range(PAGES_PER_STEP):
        vpt = v_refs[j][...].reshape(M_W, PAGE_SIZE).astype(BF16)
        acc = acc + _dot_nt(p_ref[:, j * PAGE_SIZE:(j + 1) * PAGE_SIZE], vpt)
    acc_scr[...] = acc

    @pl.when(step == pl.num_programs(1) - 1)
    def _():
        R = acc.shape[0]
        picked = jnp.where(_own_head(acc.shape), acc, 0.0)
        o_ref[...] = jnp.sum(picked.reshape(R // M_HEADS, M_HEADS, M_W), axis=1).astype(o_ref.dtype)


def _moba_values(page_table, probs, own, cache_vt, layer):
    DB, n_pages = page_table.shape
    R = probs.shape[1]
    steps = n_pages // PAGES_PER_STEP
    grid_spec = pltpu.PrefetchScalarGridSpec(
        num_scalar_prefetch=1,
        grid=(DB, steps),
        in_specs=[pl.BlockSpec((None, R, PAGES_PER_STEP * PAGE_SIZE), lambda b, s, pt: (b, 0, s)),
                  pl.BlockSpec((None, R, M_W), lambda b, s, pt: (b, 0, 0))] + _page_specs(layer),
        out_specs=pl.BlockSpec((None, R // M_HEADS, M_W), lambda b, s, pt: (b, 0, 0)),
        scratch_shapes=[pltpu.VMEM((R, M_W), F32)],
    )
    return pl.pallas_call(
        _moba_values_kernel,
        grid_spec=grid_spec,
        out_shape=jax.ShapeDtypeStruct((DB, R // M_HEADS, M_W), F32),
        compiler_params=_params(2),
        name="moba_sample_values",
    )(page_table, probs, own, *([cache_vt] * PAGES_PER_STEP))


def _merge_kernel(x_ref, ro_ref, mo_ref, gr_ref, ga_ref, wr_ref, wa_ref, wo_ref, y_ref):
    merged = (jax.nn.sigmoid(gr_ref[...]) * _dot(ro_ref[...].astype(BF16), wr_ref[...])
              + jax.nn.sigmoid(ga_ref[...]) * _dot(mo_ref[...].astype(BF16), wa_ref[...]))
    y_ref[...] = x_ref[...] + _dot(merged.astype(BF16), wo_ref[...])


def _merge(x, ro, mo, gr, ga, wr, wa, wo, tm):
    T = x.shape[0]
    row = lambda i: (i, 0)
    blk = lambda w: pl.BlockSpec((tm, w), row)
    return pl.pallas_call(
        _merge_kernel,
        grid=(T // tm,),
        in_specs=[blk(D_MODEL), blk(R_V), blk(M_W), blk(D_MODEL), blk(D_MODEL),
                  _resident(wr.shape), _resident(wa.shape), _resident(wo.shape)],
        out_specs=blk(D_MODEL),
        out_shape=jax.ShapeDtypeStruct((T, D_MODEL), F32),
        compiler_params=_params(1),
        name="merge",
    )(x, ro, mo, gr, ga, wr, wa, wo)


FF_CHUNK = 256


def _ffn_kernel(x_ref, g_ref, wgu_ref, wd_ref, gf_ref, y_ref, h_scr, act_scr, *, final_norm):
    x = x_ref[...]
    h_scr[...] = _rms(x, g_ref[...]).astype(BF16)
    for c0 in range(0, D_FF, FF_CHUNK):
        gate = _dot(h_scr[...], wgu_ref[:, c0:c0 + FF_CHUNK])
        up = _dot(h_scr[...], wgu_ref[:, D_FF + c0:D_FF + c0 + FF_CHUNK])
        act_scr[:, c0:c0 + FF_CHUNK] = (gate * jax.nn.sigmoid(gate) * up).astype(BF16)
    y = x + _dot(act_scr[...], wd_ref[...])
    if final_norm:
        y = _rms(y, gf_ref[...])
    y_ref[...] = y


def _ffn(x, g, wgu, wd, g_final, tm, final_norm):
    T = x.shape[0]
    row = lambda i: (i, 0)
    return pl.pallas_call(
        functools.partial(_ffn_kernel, final_norm=final_norm),
        grid=(T // tm,),
        in_specs=[pl.BlockSpec((tm, D_MODEL), row), _resident((1, D_MODEL)),
                  _resident(wgu.shape), _resident(wd.shape), _resident((1, D_MODEL))],
        out_specs=pl.BlockSpec((tm, D_MODEL), row),
        out_shape=jax.ShapeDtypeStruct((T, D_MODEL), F32),
        scratch_shapes=[pltpu.VMEM((tm, D_MODEL), BF16), pltpu.VMEM((tm, D_FF), BF16)],
        compiler_params=_params(1),
        name="ffn",
    )(x, g, wgu, wd, g_final)


def _rotation_tables(pos):
    posf = pos.astype(F32)
    angle = 1.0 / (10000.0 ** jnp.linspace(0.0, 1.0, R_DK // 2, dtype=F32))
    ang = jnp.repeat(posf[:, None] * angle[None, :], 2, axis=-1)
    sign = jnp.where(jnp.arange(R_DK) % 2 == 0, -1.0, 1.0).astype(F32)
    half = M_DH // 2
    inv = ROPE_THETA ** (-jnp.arange(half, dtype=F32) / half)
    ang_m = (posf[:, None] * inv[None, :]).T
    return jnp.cos(ang), jnp.sin(ang) * sign[None, :], jnp.cos(ang_m), jnp.sin(ang_m)


def _layer_tail(x, ro, mo, gr, ga, wts, l, tm, final_norm):
    x = _merge(x, ro, mo, gr, ga, wts["w_br_ret"][l], wts["w_br_att"][l], wts["w_out"][l], tm)
    return _ffn(x, wts["g_ffn"][l], wts["w_gu"][l], wts["w_down"][l], wts["g_final"], tm, final_norm)


def _feature_major_rows(t, n_seq):
    depth, _, _, S = t.shape
    return t.reshape(depth, n_seq, M_HEADS, M_DH, S).transpose(0, 1, 4, 2, 3)


def _prompt_group(x_prompt, wts):
    B, S, _ = x_prompt.shape
    depth = wts["w_in"].shape[0]
    tm = 512
    x = x_prompt.reshape(B * S, D_MODEL)
    rot = _rotation_tables(jnp.arange(S))
    ret_tabs = _retention_tables(R_CHUNK, float(R_CHUNK))
    ks, vs, states = [], [], []
    for l in range(depth):
        rq, rk, rv, rg, gr, ga, mqt, mkt, mvt = _in_proj(
            x, wts["g_mix"][l], wts["w_in"][l], wts["w_mt"][l], rot, B, tm, BF16)
        ro, st = _retention_prompt(rq, rk, rv, rg, ret_tabs, B, S)
        mo = _moba_prompt(mqt, mkt, mvt)
        x = _layer_tail(x, ro, mo, gr, ga, wts, l, tm, l == depth - 1)
        ks.append(mkt)
        vs.append(mvt)
        states.append(st)
    return (x.reshape(B, S, D_MODEL), _feature_major_rows(jnp.stack(ks), B),
            _feature_major_rows(jnp.stack(vs), B), jnp.stack(states))


def _sample_group(x_sample, cache_k, cache_v, state_ret, page_table, wts):
    DB, L, _ = x_sample.shape
    depth = wts["w_in"].shape[0]
    T = DB * L
    LP = 8
    past_len = page_table.shape[1] * PAGE_SIZE
    x = x_sample.reshape(T, D_MODEL)
    rot = _rotation_tables(past_len + (jnp.arange(T) % L))
    ret_tabs = _retention_tables(LP, float(L))
    ckt = cache_k.transpose(0, 1, 3, 4, 2)
    cvt = cache_v.transpose(0, 1, 3, 4, 2)

    def per_seq(t):
        return jnp.pad(t.reshape(DB, L, -1), ((0, 0), (0, LP - L), (0, 0)))

    ks, vs, states = [], [], []
    for l in range(depth):
        rq, rk, rv, rg, gr, ga, mqt, mkt, mvt = _in_proj(
            x, wts["g_mix"][l], wts["w_in"][l], wts["w_mt"][l], rot, 1, T, F32)
        mq, mk, mv = mqt[0].T, mkt[0].T, mvt[0].T
        ro, st = _retention_sample(per_seq(rq), per_seq(rk), per_seq(rv), per_seq(rg),
                                   state_ret, l, ret_tabs)
        q_rep = jnp.repeat(mq.reshape(DB, L, M_W), M_HEADS, axis=1)
        scores, bsum_t = _moba_keys(page_table, q_rep, ckt, l)
        probs, own = _moba_probs(scores, bsum_t, q_rep, per_seq(mk), per_seq(mv))
        mo = _moba_values(page_table, probs, own, cvt, l)
        x = _layer_tail(x, ro[:, :L].reshape(T, R_V), mo.reshape(T, M_W), gr, ga, wts, l, T,
                        l == depth - 1)
        ks.append(mk.reshape(DB, L, M_HEADS, M_DH))
        vs.append(mv.reshape(DB, L, M_HEADS, M_DH))
        states.append(st)
    return x.reshape(DB, L, D_MODEL), jnp.stack(ks), jnp.stack(vs), jnp.stack(states)


def kernel(x_prompt, x_sample, cache_k, cache_v, state_ret, page_table, g_mix, w_in, w_br_ret,
           w_br_att, w_out, g_ffn, w_gu, w_down, g_final):
    depth = w_in.shape[0]
    w_in_b = w_in.astype(BF16)
    wts = {
        "g_mix": g_mix.reshape(depth, 1, D_MODEL), "g_ffn": g_ffn.reshape(depth, 1, D_MODEL),
        "g_final": g_final.reshape(1, D_MODEL),
        "w_in": w_in_b, "w_mt": w_in_b[:, :, C_MQ:C_MV_END].transpose(0, 2, 1),
        "w_br_ret": w_br_ret.astype(BF16), "w_br_att": w_br_att.astype(BF16),
        "w_out": w_out.astype(BF16), "w_gu": w_gu.astype(BF16), "w_down": w_down.astype(BF16),
    }
    y_p, k_p, v_p, st_p = _prompt_group(x_prompt, wts)
    y_s, k_s, v_s, st_s = _sample_group(x_sample, cache_k, cache_v, state_ret, page_table, wts)
    return (y_p, y_s, k_p, v_p, st_p.astype(state_ret.dtype), k_s, v_s, st_s.astype(state_ret.dtype))
```

```python
import functools

import jax
import jax.numpy as jnp
from jax import lax
from jax.experimental import pallas as pl
from jax.experimental.pallas import tpu as pltpu

F32 = jnp.float32
BF16 = jnp.bfloat16

D_MODEL = 1024
R_HEADS = 4
R_DK = 128
R_DV = 256
R_CHUNK = 128
R_QK = R_HEADS * R_DK
R_V = R_HEADS * R_DV
M_HEADS = 8
M_DH = 64
M_W = M_HEADS * M_DH
M_BLOCK = 256
M_TOPK = 3
PAGE_SIZE = 128
ROPE_THETA = 10000.0
D_FF = 2816
NORM_EPS = 1e-6
NEG = -1e30

C_RQ, C_RK, C_RV, C_RG = 0, 512, 1024, 2048
C_MQ, C_MV_END = 3072, 4608
C_GR, C_GA = 4608, 5632
IN_WIDTH = 6656

LANES = 128
VMEM_LIMIT = 56 * 1024 * 1024
HIGHEST = lax.Precision.HIGHEST

NT_DIMS = (((1,), (1,)), ((), ()))
TN_DIMS = (((0,), (0,)), ((), ()))


def _params(n_axes):
    return pltpu.CompilerParams(
        dimension_semantics=("arbitrary",) * n_axes, vmem_limit_bytes=VMEM_LIMIT)


def _dot(a, b, precision=None):
    return jnp.dot(a, b, precision=precision, preferred_element_type=F32)


def _dot_nt(a, b):
    return lax.dot_general(a, b, NT_DIMS, preferred_element_type=F32)


def _dot_tn(a, b):
    return lax.dot_general(a, b, TN_DIMS, preferred_element_type=F32)


def _resident(shape):
    nd = len(shape)
    return pl.BlockSpec(shape, lambda *_: (0,) * nd, pipeline_mode=pl.Buffered(1))


def _rms(x, g):
    return x * lax.rsqrt(jnp.mean(x * x, axis=-1, keepdims=True) + NORM_EPS) * g


def _in_proj_kernel(x_ref, g_ref, w_ref, wmt_ref, cr_ref, sr_ref, cmt_ref, smt_ref,
                    rq_ref, rk_ref, rv_ref, rg_ref, gr_ref, ga_ref, mqt_ref, mkt_ref, mvt_ref,
                    h_scr):
    h_scr[...] = _rms(x_ref[...], g_ref[...]).astype(BF16)

    def seg(c0, width=512):
        return _dot(h_scr[...], w_ref[:, c0:c0 + width])

    tm = h_scr.shape[0]
    even_lane = (lax.broadcasted_iota(jnp.int32, (tm, LANES), 1) % 2) == 0
    cr, sr = cr_ref[...], sr_ref[...]

    for out_ref, c0, scale in ((rq_ref, C_RQ, None), (rk_ref, C_RK, R_DK ** -0.5)):
        a = seg(c0)
        for hh in range(R_HEADS):
            sl = slice(hh * LANES, (hh + 1) * LANES)
            ah = a[:, sl]
            partner = jnp.where(even_lane, pltpu.roll(ah, LANES - 1, 1), pltpu.roll(ah, 1, 1))
            r = ah * cr + partner * sr
            if scale is not None:
                r = r * scale
            out_ref[:, sl] = r.astype(out_ref.dtype)
    for j in range(2):
        cols = slice(j * 512, (j + 1) * 512)
        rv_ref[:, cols] = seg(C_RV + j * 512).astype(rv_ref.dtype)
        rg_ref[:, cols] = seg(C_RG + j * 512).astype(rg_ref.dtype)
        gr_ref[:, cols] = seg(C_GR + j * 512).astype(gr_ref.dtype)
        ga_ref[:, cols] = seg(C_GA + j * 512).astype(ga_ref.dtype)

    cmt, smt = cmt_ref[...], smt_ref[...]
    half = M_DH // 2
    for idx, out_ref in enumerate((mqt_ref, mkt_ref, mvt_ref)):
        at = _dot_nt(wmt_ref[idx * M_W:(idx + 1) * M_W, :], h_scr[...])
        if idx == 2:
            out_ref[...] = at
            continue
        for hd in range(M_HEADS):
            x1 = at[hd * M_DH:hd * M_DH + half]
            x2 = at[hd * M_DH + half:(hd + 1) * M_DH]
            out_ref[hd * M_DH:hd * M_DH + half, :] = x1 * cmt - x2 * smt
            out_ref[hd * M_DH + half:(hd + 1) * M_DH, :] = x2 * cmt + x1 * smt


def _in_proj(x, g, w, wmt, tabs, n_seq, tm, r_dtype):
    T = x.shape[0]
    S = T // n_seq
    per_seq = S // tm
    cr, sr, cmt, smt = tabs
    row = lambda i: (i, 0)
    tab_spec = pl.BlockSpec((tm, LANES), lambda i: (i % per_seq, 0))
    tabt_spec = pl.BlockSpec((M_DH // 2, tm), lambda i: (0, i % per_seq))
    widths = (R_QK, R_QK, R_V, R_V, D_MODEL, D_MODEL)
    dtypes = (r_dtype, r_dtype, r_dtype, F32, F32, F32)
    featmajor = pl.BlockSpec((None, M_W, tm), lambda i: (i // per_seq, 0, i % per_seq))
    return pl.pallas_call(
        _in_proj_kernel,
        grid=(T // tm,),
        in_specs=[pl.BlockSpec((tm, D_MODEL), row), _resident((1, D_MODEL)),
                  _resident(w.shape), _resident(wmt.shape),
                  tab_spec, tab_spec, tabt_spec, tabt_spec],
        out_specs=[pl.BlockSpec((tm, wd), row) for wd in widths] + [featmajor] * 3,
        out_shape=[jax.ShapeDtypeStruct((T, wd), dt) for wd, dt in zip(widths, dtypes)]
        + [jax.ShapeDtypeStruct((n_seq, M_W, S), F32)] * 3,
        scratch_shapes=[pltpu.VMEM((tm, D_MODEL), BF16)],
        compiler_params=_params(1),
        name="in_proj",
    )(x, g, w, wmt, cr, sr, cmt, smt)


def _head_rms_gate(o, gate):
    on = o * lax.rsqrt(jnp.mean(o * o, axis=-1, keepdims=True) + NORM_EPS)
    return on * (gate * jax.nn.sigmoid(gate))


def _retention_head(q, k, v, s_prev, dec, qd, kd, gl):
    qf, kf = q.astype(F32), k.astype(F32)
    vb = v.astype(BF16)
    inner = _dot_nt(q.astype(BF16), k.astype(BF16)) * dec
    o = _dot(inner.astype(BF16), vb) + _dot((qf * qd).astype(BF16), s_prev.astype(BF16))
    s_new = gl * s_prev + _dot_tn((kf * kd).astype(BF16), vb)
    return o, s_new


def _ret_prompt_kernel(q_ref, k_ref, v_ref, g_ref, dec_ref, qd_ref, kd_ref, gl_ref,
                       o_ref, st_ref, s_scr):
    c = pl.program_id(1)

    @pl.when(c == 0)
    def _():
        s_scr[...] = jnp.zeros_like(s_scr)

    for hh in range(R_HEADS):
        ks = slice(hh * R_DK, (hh + 1) * R_DK)
        vs = slice(hh * R_DV, (hh + 1) * R_DV)
        o, s_new = _retention_head(q_ref[:, ks], k_ref[:, ks], v_ref[:, vs], s_scr[hh],
                                   dec_ref[hh], qd_ref[hh], kd_ref[hh], gl_ref[hh])
        s_scr[hh] = s_new
        o_ref[:, vs] = _head_rms_gate(o, g_ref[:, vs]).astype(o_ref.dtype)

    @pl.when(c == pl.num_programs(1) - 1)
    def _():
        st_ref[...] = s_scr[...]


def _retention_prompt(rq, rk, rv, rg, tabs, B, S):
    nc = S // R_CHUNK
    row = lambda b, c: (b * nc + c, 0)
    dec, qd, kd, gl = tabs
    return pl.pallas_call(
        _ret_prompt_kernel,
        grid=(B, nc),
        in_specs=[pl.BlockSpec((R_CHUNK, R_QK), row), pl.BlockSpec((R_CHUNK, R_QK), row),
                  pl.BlockSpec((R_CHUNK, R_V), row), pl.BlockSpec((R_CHUNK, R_V), row),
                  _resident(dec.shape), _resident(qd.shape), _resident(kd.shape),
                  _resident(gl.shape)],
        out_specs=[pl.BlockSpec((R_CHUNK, R_V), row),
                   pl.BlockSpec((None, R_HEADS, R_DK, R_DV), lambda b, c: (b, 0, 0, 0))],
        out_shape=[jax.ShapeDtypeStruct((B * S, R_V), BF16),
                   jax.ShapeDtypeStruct((B, R_HEADS, R_DK, R_DV), F32)],
        scratch_shapes=[pltpu.VMEM((R_HEADS, R_DK, R_DV), F32)],
        compiler_params=_params(2),
        name="retention_prompt",
    )(rq, rk, rv, rg, dec, qd, kd, gl)


def _ret_sample_kernel(q_ref, k_ref, v_ref, g_ref, s_ref, dec_ref, qd_ref, kd_ref, gl_ref,
                       o_ref, st_ref):
    for hh in range(R_HEADS):
        ks = slice(hh * R_DK, (hh + 1) * R_DK)
        vs = slice(hh * R_DV, (hh + 1) * R_DV)
        o, s_new = _retention_head(q_ref[:, ks], k_ref[:, ks], v_ref[:, vs], s_ref[hh],
                                   dec_ref[hh], qd_ref[hh], kd_ref[hh], gl_ref[hh])
        st_ref[hh] = s_new
        o_ref[:, vs] = _head_rms_gate(o, g_ref[:, vs]).astype(o_ref.dtype)


def _retention_sample(rq, rk, rv, rg, state_all, layer, tabs):
    DB, LP = rq.shape[:2]
    dec, qd, kd, gl = tabs
    tok = lambda w: pl.BlockSpec((None, LP, w), lambda b: (b, 0, 0))
    st_in = pl.BlockSpec((None, None, R_HEADS, R_DK, R_DV), lambda b: (layer, b, 0, 0, 0))
    return pl.pallas_call(
        _ret_sample_kernel,
        grid=(DB,),
        in_specs=[tok(R_QK), tok(R_QK), tok(R_V), tok(R_V), st_in,
                  _resident(dec.shape), _resident(qd.shape), _resident(kd.shape),
                  _resident(gl.shape)],
        out_specs=[tok(R_V),
                   pl.BlockSpec((None, R_HEADS, R_DK, R_DV), lambda b: (b, 0, 0, 0))],
        out_shape=[jax.ShapeDtypeStruct((DB, LP, R_V), F32),
                   jax.ShapeDtypeStruct((DB, R_HEADS, R_DK, R_DV), F32)],
        compiler_params=_params(1),
        name="retention_sample",
    )(rq, rk, rv, rg, state_all, dec, qd, kd, gl)


def _retention_tables(L, l_real):
    log_gamma = jnp.log1p(-jnp.exp2(-5.0 - jnp.arange(R_HEADS, dtype=F32)))
    n = jnp.arange(L, dtype=F32)
    diff = n[:, None] - n[None, :]
    dec = jnp.where(diff >= 0, jnp.exp(log_gamma[:, None, None] * jnp.maximum(diff, 0.0)), 0.0)
    q_decay = jnp.exp(log_gamma[:, None] * (n[None, :] + 1.0))
    k_decay = jnp.exp(log_gamma[:, None] * (l_real - 1.0 - n[None, :]))
    qd = jnp.broadcast_to(q_decay[:, :, None], (R_HEADS, L, R_DK))
    kd = jnp.broadcast_to(k_decay[:, :, None], (R_HEADS, L, R_DK))
    gl = jnp.broadcast_to(jnp.exp(log_gamma * l_real)[:, None, None], (R_HEADS, 1, R_DV))
    return dec, qd, kd, gl


def _rank_select(scores, n_idx, n_cand, axis):
    nb = scores.shape[axis]
    cnt = jnp.zeros(scores.shape, F32)
    for m in range(nb):
        sm = scores[m:m + 1, :] if axis == 0 else scores[:, m:m + 1]
        beats = (sm > scores) | ((sm == scores) & (m < n_idx))
        cnt = cnt + jnp.where(beats, jnp.where(m < n_cand, 1.0, 0.0), 0.0)
    return jnp.where((n_idx < n_cand) & (cnt < float(M_TOPK)), 1.0, 0.0)


BF16_ROWS = 16
V_ROWS = M_DH + BF16_ROWS
LOG2E = 1.4426950408889634
CHUNK_BLOCKS = 4


def _moba_prompt_kernel(qt_ref, kt_ref, vt_ref, o_ref, kaug_scr, vaug_scr, vblk_scr, means_scr,
                        s_scr, sd_scr, m_scr, acc_scr):
    i = pl.program_id(2)
    nb = means_scr.shape[0]
    tq = qt_ref.shape[1]
    ck = CHUNK_BLOCKS * M_BLOCK
    own_rows = (slice(0, M_DH), slice(M_DH, LANES))
    onehot_lane0 = (M_DH, 0)

    @pl.when(i == 0)
    def _():
        lane = lax.broadcasted_iota(jnp.int32, (M_BLOCK, LANES), 1)
        ones_tile = jnp.ones((BF16_ROWS, M_BLOCK), BF16)
        for n in range(nb):
            cols = slice(n * M_BLOCK, (n + 1) * M_BLOCK)
            kblk = kt_ref[:, cols].T
            means_scr[n:n + 1, :] = jnp.sum(kblk, axis=0, keepdims=True) * (1.0 / M_BLOCK)
            for hd in range(2):
                mine = (lane < M_DH) if hd == 0 else (lane >= M_DH)
                other = jnp.where(lane == onehot_lane0[hd] + n, 1.0, 0.0)
                kaug_scr[hd, cols, :] = jnp.where(mine, kblk, other).astype(BF16)
                vb = vt_ref[own_rows[hd], cols].astype(BF16)
                vaug_scr[hd, 0:M_DH, cols] = vb
                vaug_scr[hd, M_DH:V_ROWS, cols] = ones_tile
                vblk_scr[hd, n, 0:M_DH, :] = vb
                vblk_scr[hd, n, M_DH:V_ROWS, :] = ones_tile

    qt = qt_ref[...]
    row = lax.broadcasted_iota(jnp.int32, qt.shape, 0)
    n_idx = lax.broadcasted_iota(jnp.int32, (nb, tq), 0)
    means = means_scr[...]
    causal = (lax.broadcasted_iota(jnp.int32, (M_BLOCK, tq), 0)
              <= lax.broadcasted_iota(jnp.int32, (M_BLOCK, tq), 1))
    own_rows_k = pl.ds(pl.multiple_of(i * M_BLOCK, M_BLOCK), M_BLOCK)
    pad = jnp.zeros((M_DH - nb, tq), BF16)
    zeros_half = jnp.zeros((M_DH, tq), BF16)
    rhs = []
    for hd in range(2):
        mine = (row < M_DH) if hd == 0 else (row >= M_DH)
        qh = jnp.where(mine, qt, 0.0)
        sel = _rank_select(_dot(means, qh, precision=HIGHEST), n_idx, i, axis=0)
        bias = jnp.where(sel > 0.5, 0.0, NEG).astype(BF16)
        q_own = (qt[own_rows[hd]] * (M_DH ** -0.5 * LOG2E)).astype(BF16)
        parts = [q_own, bias, pad]
        rhs.append(jnp.concatenate(parts if hd == 0 else parts[1:] + parts[:1], axis=0))
        q_only = jnp.concatenate([q_own, zeros_half] if hd == 0 else [zeros_half, q_own], axis=0)
        s_d = jnp.where(causal, _dot(kaug_scr[hd, own_rows_k, :], q_only), NEG)
        sd_scr[hd] = s_d
        m_scr[hd] = jnp.max(s_d, axis=0, keepdims=True)

    for c in range(nb // CHUNK_BLOCKS):
        @pl.when(c * CHUNK_BLOCKS < i)
        def _():
            for hd in range(2):
                s = _dot(kaug_scr[hd, c * ck:(c + 1) * ck, :], rhs[hd])
                s_scr[hd, c] = s
                m_scr[hd] = jnp.maximum(m_scr[hd], jnp.max(s, axis=0, keepdims=True))

    for hd in range(2):
        p_d = jnp.exp2(sd_scr[hd] - m_scr[hd]).astype(BF16)
        acc_scr[hd] = _dot(vblk_scr[hd, i], p_d)

    for c in range(nb // CHUNK_BLOCKS):
        @pl.when(c * CHUNK_BLOCKS < i)
        def _():
            for hd in range(2):
                p = jnp.exp2(s_scr[hd, c] - m_scr[hd]).astype(BF16)
                acc_scr[hd] = acc_scr[hd] + _dot(vaug_scr[hd, :, c * ck:(c + 1) * ck], p)

    o_t = jnp.concatenate([acc_scr[0, 0:M_DH, :] / acc_scr[0, M_DH:M_DH + 1, :],
                           acc_scr[1, 0:M_DH, :] / acc_scr[1, M_DH:M_DH + 1, :]], axis=0)
    o_ref[...] = o_t.T.astype(o_ref.dtype)


def _moba_prompt(mqt, mkt, mvt):
    B, _, S = mqt.shape
    nb = S // M_BLOCK
    tq = M_BLOCK
    npair = M_W // LANES
    assert nb % CHUNK_BLOCKS == 0 and nb <= M_DH and nb % BF16_ROWS == 0
    return pl.pallas_call(
        _moba_prompt_kernel,
        grid=(B, npair, nb),
        in_specs=[pl.BlockSpec((None, LANES, tq), lambda b, p, i: (b, p, i)),
                  pl.BlockSpec((None, LANES, S), lambda b, p, i: (b, p, 0)),
                  pl.BlockSpec((None, LANES, S), lambda b, p, i: (b, p, 0))],
        out_specs=pl.BlockSpec((tq, LANES), lambda b, p, i: (b * nb + i, p)),
        out_shape=jax.ShapeDtypeStruct((B * S, M_W), BF16),
        scratch_shapes=[pltpu.VMEM((2, S, LANES), BF16),
                        pltpu.VMEM((2, V_ROWS, S), BF16),
                        pltpu.VMEM((2, nb, V_ROWS, M_BLOCK), BF16),
                        pltpu.VMEM((nb, LANES), F32),
                        pltpu.VMEM((2, nb // CHUNK_BLOCKS, CHUNK_BLOCKS * M_BLOCK, tq), F32),
                        pltpu.VMEM((2, M_BLOCK, tq), F32),
                        pltpu.VMEM((2, 1, tq), F32),
                        pltpu.VMEM((2, V_ROWS, tq), F32)],
        compiler_params=_params(3),
        name="moba_prompt",
    )(mqt, mkt, mvt)


PAGES_PER_STEP = 16


def _own_head(shape):
    row = lax.broadcasted_iota(jnp.int32, shape, 0)
    lane = lax.broadcasted_iota(jnp.int32, shape, 1)
    return (lane // M_DH) == (row % M_HEADS)


def _moba_keys_kernel(pt_ref, q_ref, *refs):
    k_refs = refs[:PAGES_PER_STEP]
    s_ref, bsum_ref = refs[PAGES_PER_STEP:]
    step = pl.program_id(1)

    @pl.when(step == 0)
    def _():
        bsum_ref[...] = jnp.zeros_like(bsum_ref)

    q = q_ref[...]
    qb = (jnp.where(_own_head(q.shape), q, 0.0) * (M_DH ** -0.5)).astype(BF16)
    ppb = M_BLOCK // PAGE_SIZE
    blk_lane = lax.broadcasted_iota(jnp.int32, bsum_ref.shape, 1)
    bsum = bsum_ref[...]
    for j in range(PAGES_PER_STEP):
        kpt = k_refs[j][...].reshape(M_W, PAGE_SIZE)
        s_ref[:, j * PAGE_SIZE:(j + 1) * PAGE_SIZE] = _dot(qb, kpt.astype(BF16))
        rowsum = jnp.sum(kpt, axis=1, keepdims=True)
        n = step * (PAGES_PER_STEP // ppb) + j // ppb
        bsum = bsum + jnp.where(blk_lane == n, rowsum, 0.0)
    bsum_ref[...] = bsum


def _page_specs(layer):
    def spec(j):
        return pl.BlockSpec(
            (None, None, M_HEADS, M_DH, PAGE_SIZE),
            lambda b, s, pt: (layer, pt[b, s * PAGES_PER_STEP + j], 0, 0, 0))
    return [spec(j) for j in range(PAGES_PER_STEP)]


def _moba_keys(page_table, q_rep, cache_kt, layer):
    DB, n_pages = page_table.shape
    R = q_rep.shape[1]
    steps = n_pages // PAGES_PER_STEP
    nb = n_pages * PAGE_SIZE // M_BLOCK
    grid_spec = pltpu.PrefetchScalarGridSpec(
        num_scalar_prefetch=1,
        grid=(DB, steps),
        in_specs=[pl.BlockSpec((None, R, M_W), lambda b, s, pt: (b, 0, 0))] + _page_specs(layer),
        out_specs=[pl.BlockSpec((None, R, PAGES_PER_STEP * PAGE_SIZE), lambda b, s, pt: (b, 0, s)),
                   pl.BlockSpec((None, M_W, nb), lambda b, s, pt: (b, 0, 0))],
    )
    return pl.pallas_call(
        _moba_keys_kernel,
        grid_spec=grid_spec,
        out_shape=[jax.ShapeDtypeStruct((DB, R, n_pages * PAGE_SIZE), F32),
                   jax.ShapeDtypeStruct((DB, M_W, nb), F32)],
        compiler_params=_params(2),
        name="moba_sample_keys",
    )(page_table, q_rep, *([cache_kt] * PAGES_PER_STEP))


def _moba_probs_kernel(s_ref, bsum_ref, q_ref, kn_ref, vn_ref, p_ref, own_ref):
    q = q_ref[...]
    R = q.shape[0]
    qh = jnp.where(_own_head(q.shape), q, 0.0)
    means_t = bsum_ref[...] * (1.0 / M_BLOCK)
    nb = means_t.shape[1]
    sc = _dot(qh, means_t, precision=HIGHEST)
    n_idx = lax.broadcasted_iota(jnp.int32, (R, nb), 1)
    sel = _rank_select(sc, n_idx, nb, axis=1)
    s = s_ref[...]
    past = s.shape[1]
    blk_of_key = lax.broadcasted_iota(jnp.int32, (nb, past), 1) // M_BLOCK
    expand = jnp.where(blk_of_key == lax.broadcasted_iota(jnp.int32, (nb, past), 0), 1.0, 0.0)
    keep = _dot(sel.astype(BF16), expand.astype(BF16)) > 0.5
    s = jnp.where(keep, s, NEG)
    qb = (qh * (M_DH ** -0.5)).astype(BF16)
    s_own = _dot_nt(qb, kn_ref[...].astype(BF16))
    tok = lax.broadcasted_iota(jnp.int32, s_own.shape, 0) // M_HEADS
    s_own = jnp.where(lax.broadcasted_iota(jnp.int32, s_own.shape, 1) <= tok, s_own, NEG)
    m = jnp.maximum(jnp.max(s, axis=1, keepdims=True), jnp.max(s_own, axis=1, keepdims=True))
    p = jnp.exp(s - m)
    p_own = jnp.exp(s_own - m)
    inv = 1.0 / (jnp.sum(p, axis=1, keepdims=True) + jnp.sum(p_own, axis=1, keepdims=True))
    p_ref[...] = (p * inv).astype(p_ref.dtype)
    own_ref[...] = _dot((p_own * inv).astype(BF16), vn_ref[...].astype(BF16))


def _moba_probs(scores, bsum_t, q_rep, k_new, v_new):
    DB, R, past = scores.shape
    nb = bsum_t.shape[2]
    LP = k_new.shape[1]
    per_b = lambda *shape: pl.BlockSpec((None,) + shape, lambda b: (b, 0, 0))
    return pl.pallas_call(
        _moba_probs_kernel,
        grid=(DB,),
        in_specs=[per_b(R, past), per_b(M_W, nb), per_b(R, M_W), per_b(LP, M_W), per_b(LP, M_W)],
        out_specs=[per_b(R, past), per_b(R, M_W)],
        out_shape=[jax.ShapeDtypeStruct((DB, R, past), BF16),
                   jax.ShapeDtypeStruct((DB, R, M_W), F32)],
        compiler_params=_params(1),
        name="moba_sample_probs",
    )(scores, bsum_t, q_rep, k_new, v_new)


def _moba_values_kernel(pt_ref, p_ref, own_ref, *refs):
    v_refs = refs[:PAGES_PER_STEP]
    o_ref, acc_scr = refs[PAGES_PER_STEP:]
    step = pl.program_id(1)

    @pl.when(step == 0)
    def _():
        acc_scr[...] = own_ref[...]

    acc = acc_scr[...]
    for j in range(PAGES_PER_STEP):
        vpt = v_refs[j][...].reshape(M_W, PAGE_SIZE).astype(BF16)
        acc = acc + _dot_nt(p_ref[:, j * PAGE_SIZE:(j + 1) * PAGE_SIZE], vpt)
    acc_scr[...] = acc

    @pl.when(step == pl.num_programs(1) - 1)
    def _():
        R = acc.shape[0]
        picked = jnp.where(_own_head(acc.shape), acc, 0.0)
        o_ref[...] = jnp.sum(picked.reshape(R // M_HEADS, M_HEADS, M_W), axis=1).astype(o_ref.dtype)


def _moba_values(page_table, probs, own, cache_vt, layer):
    DB, n_pages = page_table.shape
    R = probs.shape[1]
    steps = n_pages // PAGES_PER_STEP
    grid_spec = pltpu.PrefetchScalarGridSpec(
        num_scalar_prefetch=1,
        grid=(DB, steps),
        in_specs=[pl.BlockSpec((None, R, PAGES_PER_STEP * PAGE_SIZE), lambda b, s, pt: (b, 0, s)),
                  pl.BlockSpec((None, R, M_W), lambda b, s, pt: (b, 0, 0))] + _page_specs(layer),
        out_specs=pl.BlockSpec((None, R // M_HEADS, M_W), lambda b, s, pt: (b, 0, 0)),
        scratch_shapes=[pltpu.VMEM((R, M_W), F32)],
    )
    return pl.pallas_call(
        _moba_values_kernel,
        grid_spec=grid_spec,
        out_shape=jax.ShapeDtypeStruct((DB, R // M_HEADS, M_W), F32),
        compiler_params=_params(2),
        name="moba_sample_values",
    )(page_table, probs, own, *([cache_vt] * PAGES_PER_STEP))


def _merge_kernel(x_ref, ro_ref, mo_ref, gr_ref, ga_ref, wr_ref, wa_ref, wo_ref, y_ref):
    merged = (jax.nn.sigmoid(gr_ref[...]) * _dot(ro_ref[...].astype(BF16), wr_ref[...])
              + jax.nn.sigmoid(ga_ref[...]) * _dot(mo_ref[...].astype(BF16), wa_ref[...]))
    y_ref[...] = x_ref[...] + _dot(merged.astype(BF16), wo_ref[...])


def _merge(x, ro, mo, gr, ga, wr, wa, wo, tm):
    T = x.shape[0]
    row = lambda i: (i, 0)
    blk = lambda w: pl.BlockSpec((tm, w), row)
    return pl.pallas_call(
        _merge_kernel,
        grid=(T // tm,),
        in_specs=[blk(D_MODEL), blk(R_V), blk(M_W), blk(D_MODEL), blk(D_MODEL),
                  _resident(wr.shape), _resident(wa.shape), _resident(wo.shape)],
        out_specs=blk(D_MODEL),
        out_shape=jax.ShapeDtypeStruct((T, D_MODEL), F32),
        compiler_params=_params(1),
        name="merge",
    )(x, ro, mo, gr, ga, wr, wa, wo)


FF_CHUNK = 256


def _ffn_kernel(x_ref, g_ref, wgu_ref, wd_ref, gf_ref, y_ref, h_scr, act_scr, *, final_norm):
    x = x_ref[...]
    h_scr[...] = _rms(x, g_ref[...]).astype(BF16)
    for c0 in range(0, D_FF, FF_CHUNK):
        gate = _dot(h_scr[...], wgu_ref[:, c0:c0 + FF_CHUNK])
        up = _dot(h_scr[...], wgu_ref[:, D_FF + c0:D_FF + c0 + FF_CHUNK])
        act_scr[:, c0:c0 + FF_CHUNK] = (gate * jax.nn.sigmoid(gate) * up).astype(BF16)
    y = x + _dot(act_scr[...], wd_ref[...])
    if final_norm:
        y = _rms(y, gf_ref[...])
    y_ref[...] = y


def _ffn(x, g, wgu, wd, g_final, tm, final_norm):
    T = x.shape[0]
    row = lambda i: (i, 0)
    return pl.pallas_call(
        functools.partial(_ffn_kernel, final_norm=final_norm),
        grid=(T // tm,),
        in_specs=[pl.BlockSpec((tm, D_MODEL), row), _resident((1, D_MODEL)),
                  _resident(wgu.shape), _resident(wd.shape), _resident((1, D_MODEL))],
        out_specs=pl.BlockSpec((tm, D_MODEL), row),
        out_shape=jax.ShapeDtypeStruct((T, D_MODEL), F32),
        scratch_shapes=[pltpu.VMEM((tm, D_MODEL), BF16), pltpu.VMEM((tm, D_FF), BF16)],
        compiler_params=_params(1),
        name="ffn",
    )(x, g, wgu, wd, g_final)


def _rotation_tables(pos):
    posf = pos.astype(F32)
    angle = 1.0 / (10000.0 ** jnp.linspace(0.0, 1.0, R_DK // 2, dtype=F32))
    ang = jnp.repeat(posf[:, None] * angle[None, :], 2, axis=-1)
    sign = jnp.where(jnp.arange(R_DK) % 2 == 0, -1.0, 1.0).astype(F32)
    half = M_DH // 2
    inv = ROPE_THETA ** (-jnp.arange(half, dtype=F32) / half)
    ang_m = (posf[:, None] * inv[None, :]).T
    return jnp.cos(ang), jnp.sin(ang) * sign[None, :], jnp.cos(ang_m), jnp.sin(ang_m)


def _layer_tail(x, ro, mo, gr, ga, wts, l, tm, final_norm):
    x = _merge(x, ro, mo, gr, ga, wts["w_br_ret"][l], wts["w_br_att"][l], wts["w_out"][l], tm)
    return _ffn(x, wts["g_ffn"][l], wts["w_gu"][l], wts["w_down"][l], wts["g_final"], tm, final_norm)


def _feature_major_rows(t, n_seq):
    depth, _, _, S = t.shape
    return t.reshape(depth, n_seq, M_HEADS, M_DH, S).transpose(0, 1, 4, 2, 3)


def _prompt_group(x_prompt, wts):
    B, S, _ = x_prompt.shape
    depth = wts["w_in"].shape[0]
    tm = 512
    x = x_prompt.reshape(B * S, D_MODEL)
    rot = _rotation_tables(jnp.arange(S))
    ret_tabs = _retention_tables(R_CHUNK, float(R_CHUNK))
    ks, vs, states = [], [], []
    for l in range(depth):
        rq, rk, rv, rg, gr, ga, mqt, mkt, mvt = _in_proj(
            x, wts["g_mix"][l], wts["w_in"][l], wts["w_mt"][l], rot, B, tm, BF16)
        ro, st = _retention_prompt(rq, rk, rv, rg, ret_tabs, B, S)
        mo = _moba_prompt(mqt, mkt, mvt)
        x = _layer_tail(x, ro, mo, gr, ga, wts, l, tm, l == depth - 1)
        ks.append(mkt)
        vs.append(mvt)
        states.append(st)
    return (x.reshape(B, S, D_MODEL), _feature_major_rows(jnp.stack(ks), B),
            _feature_major_rows(jnp.stack(vs), B), jnp.stack(states))


def _sample_group(x_sample, cache_k, cache_v, state_ret, page_table, wts):
    DB, L, _ = x_sample.shape
    depth = wts["w_in"].shape[0]
    T = DB * L
    LP = 8
    past_len = page_table.shape[1] * PAGE_SIZE
    x = x_sample.reshape(T, D_MODEL)
    rot = _rotation_tables(past_len + (jnp.arange(T) % L))
    ret_tabs = _retention_tables(LP, float(L))
    ckt = cache_k.transpose(0, 1, 3, 4, 2)
    cvt = cache_v.transpose(0, 1, 3, 4, 2)

    def per_seq(t):
        return jnp.pad(t.reshape(DB, L, -1), ((0, 0), (0, LP - L), (0, 0)))

    ks, vs, states = [], [], []
    for l in range(depth):
        rq, rk, rv, rg, gr, ga, mqt, mkt, mvt = _in_proj(
            x, wts["g_mix"][l], wts["w_in"][l], wts["w_mt"][l], rot, 1, T, F32)
        mq, mk, mv = mqt[0].T, mkt[0].T, mvt[0].T
        ro, st = _retention_sample(per_seq(rq), per_seq(rk), per_seq(rv), per_seq(rg),
                                   state_ret, l, ret_tabs)
        q_rep = jnp.repeat(mq.reshape(DB, L, M_W), M_HEADS, axis=1)
        scores, bsum_t = _moba_keys(page_table, q_rep, ckt, l)
        probs, own = _moba_probs(scores, bsum_t, q_rep, per_seq(mk), per_seq(mv))
        mo = _moba_values(page_table, probs, own, cvt, l)
        x = _layer_tail(x, ro[:, :L].reshape(T, R_V), mo.reshape(T, M_W), gr, ga, wts, l, T,
                        l == depth - 1)
        ks.append(mk.reshape(DB, L, M_HEADS, M_DH))
        vs.append(mv.reshape(DB, L, M_HEADS, M_DH))
        states.append(st)
    return x.reshape(DB, L, D_MODEL), jnp.stack(ks), jnp.stack(vs), jnp.stack(states)


def kernel(x_prompt, x_sample, cache_k, cache_v, state_ret, page_table, g_mix, w_in, w_br_ret,
           w_br_att, w_out, g_ffn, w_gu, w_down, g_final):
    depth = w_in.shape[0]
    w_in_b = w_in.astype(BF16)
    wts = {
        "g_mix": g_mix.reshape(depth, 1, D_MODEL), "g_ffn": g_ffn.reshape(depth, 1, D_MODEL),
        "g_final": g_final.reshape(1, D_MODEL),
        "w_in": w_in_b, "w_mt": w_in_b[:, :, C_MQ:C_MV_END].transpose(0, 2, 1),
        "w_br_ret": w_br_ret.astype(BF16), "w_br_att": w_br_att.astype(BF16),
        "w_out": w_out.astype(BF16), "w_gu": w_gu.astype(BF16), "w_down": w_down.astype(BF16),
    }
    y_p, k_p, v_p, st_p = _prompt_group(x_prompt, wts)
    y_s, k_s, v_s, st_s = _sample_group(x_sample, cache_k, cache_v, state_ret, page_table, wts)
    return (y_p, y_s, k_p, v_p, st_p.astype(state_ret.dtype), k_s, v_s, st_s.astype(state_ret.dtype))
```

```python
import functools

import jax
import jax.numpy as jnp
from jax import lax
from jax.experimental import pallas as pl
from jax.experimental.pallas import tpu as pltpu

F32 = jnp.float32
BF16 = jnp.bfloat16

D_MODEL = 1024
R_HEADS = 4
R_DK = 128
R_DV = 256
R_CHUNK = 128
R_QK = R_HEADS * R_DK
R_V = R_HEADS * R_DV
M_HEADS = 8
M_DH = 64
M_W = M_HEADS * M_DH
M_BLOCK = 256
M_TOPK = 3
PAGE_SIZE = 128
ROPE_THETA = 10000.0
D_FF = 2816
NORM_EPS = 1e-6
NEG = -1e30

C_RQ, C_RK, C_RV, C_RG = 0, 512, 1024, 2048
C_MQ = 3072
C_GR, C_GA = 4608, 5632
IN_WIDTH = 6656

LANES = 128
VMEM_LIMIT = 56 * 1024 * 1024
HIGHEST = lax.Precision.HIGHEST

NT_DIMS = (((1,), (1,)), ((), ()))
TN_DIMS = (((0,), (0,)), ((), ()))


def _params(n_axes):
    return pltpu.CompilerParams(
        dimension_semantics=("arbitrary",) * n_axes, vmem_limit_bytes=VMEM_LIMIT)


def _dot(a, b, precision=None):
    return jnp.dot(a, b, precision=precision, preferred_element_type=F32)


def _dot_nt(a, b):
    return lax.dot_general(a, b, NT_DIMS, preferred_element_type=F32)


def _dot_tn(a, b):
    return lax.dot_general(a, b, TN_DIMS, preferred_element_type=F32)


def _resident(shape):
    nd = len(shape)
    return pl.BlockSpec(shape, lambda *_: (0,) * nd, pipeline_mode=pl.Buffered(1))


def _rms(x, g):
    return x * lax.rsqrt(jnp.mean(x * x, axis=-1, keepdims=True) + NORM_EPS) * g


def _in_proj_kernel(x_ref, g_ref, w_ref, cr_ref, sr_ref, cmt_ref, smt_ref,
                    rq_ref, rk_ref, rv_ref, rg_ref, gr_ref, ga_ref, mqt_ref, mkt_ref, mvt_ref,
                    h_scr):
    h_scr[...] = _rms(x_ref[...], g_ref[...]).astype(BF16)

    def seg(c0, width=512):
        return _dot(h_scr[...], w_ref[:, c0:c0 + width])

    tm = h_scr.shape[0]
    even_lane = (lax.broadcasted_iota(jnp.int32, (tm, LANES), 1) % 2) == 0
    cr, sr = cr_ref[...], sr_ref[...]

    for out_ref, c0, scale in ((rq_ref, C_RQ, None), (rk_ref, C_RK, R_DK ** -0.5)):
        a = seg(c0)
        for hh in range(R_HEADS):
            sl = slice(hh * LANES, (hh + 1) * LANES)
            ah = a[:, sl]
            partner = jnp.where(even_lane, pltpu.roll(ah, LANES - 1, 1), pltpu.roll(ah, 1, 1))
            r = ah * cr + partner * sr
            if scale is not None:
                r = r * scale
            out_ref[:, sl] = r.astype(out_ref.dtype)
    for j in range(2):
        cols = slice(j * 512, (j + 1) * 512)
        rv_ref[:, cols] = seg(C_RV + j * 512).astype(rv_ref.dtype)
        rg_ref[:, cols] = seg(C_RG + j * 512).astype(rg_ref.dtype)
        gr_ref[:, cols] = seg(C_GR + j * 512).astype(gr_ref.dtype)
        ga_ref[:, cols] = seg(C_GA + j * 512).astype(ga_ref.dtype)

    cmt, smt = cmt_ref[...], smt_ref[...]
    half = M_DH // 2
    for idx, out_ref in enumerate((mqt_ref, mkt_ref, mvt_ref)):
        at = seg(C_MQ + idx * M_W).T
        if idx == 2:
            out_ref[...] = at
            continue
        for hd in range(M_HEADS):
            x1 = at[hd * M_DH:hd * M_DH + half]
            x2 = at[hd * M_DH + half:(hd + 1) * M_DH]
            out_ref[hd * M_DH:hd * M_DH + half, :] = x1 * cmt - x2 * smt
            out_ref[hd * M_DH + half:(hd + 1) * M_DH, :] = x2 * cmt + x1 * smt


def _in_proj(x, g, w, tabs, n_seq, tm, r_dtype):
    T = x.shape[0]
    S = T // n_seq
    per_seq = S // tm
    cr, sr, cmt, smt = tabs
    row = lambda i: (i, 0)
    tab_spec = pl.BlockSpec((tm, LANES), lambda i: (i % per_seq, 0))
    tabt_spec = pl.BlockSpec((M_DH // 2, tm), lambda i: (0, i % per_seq))
    widths = (R_QK, R_QK, R_V, R_V, D_MODEL, D_MODEL)
    dtypes = (r_dtype, r_dtype, r_dtype, F32, F32, F32)
    featmajor = pl.BlockSpec((None, M_W, tm), lambda i: (i // per_seq, 0, i % per_seq))
    return pl.pallas_call(
        _in_proj_kernel,
        grid=(T // tm,),
        in_specs=[pl.BlockSpec((tm, D_MODEL), row), _resident((1, D_MODEL)),
                  _resident(w.shape), tab_spec, tab_spec, tabt_spec, tabt_spec],
        out_specs=[pl.BlockSpec((tm, wd), row) for wd in widths] + [featmajor] * 3,
        out_shape=[jax.ShapeDtypeStruct((T, wd), dt) for wd, dt in zip(widths, dtypes)]
        + [jax.ShapeDtypeStruct((n_seq, M_W, S), F32)] * 3,
        scratch_shapes=[pltpu.VMEM((tm, D_MODEL), BF16)],
        compiler_params=_params(1),
        name="in_proj",
    )(x, g, w, cr, sr, cmt, smt)


def _head_rms_gate(o, gate):
    on = o * lax.rsqrt(jnp.mean(o * o, axis=-1, keepdims=True) + NORM_EPS)
    return on * (gate * jax.nn.sigmoid(gate))


def _retention_head(q, k, v, s_prev, dec, qd, kd, gl):
    qf, kf = q.astype(F32), k.astype(F32)
    vb = v.astype(BF16)
    inner = _dot_nt(q.astype(BF16), k.astype(BF16)) * dec
    o = _dot(inner.astype(BF16), vb) + _dot((qf * qd).astype(BF16), s_prev.astype(BF16))
    s_new = gl * s_prev + _dot_tn((kf * kd).astype(BF16), vb)
    return o, s_new


RET_CHUNKS_PER_STEP = 4


def _ret_prompt_kernel(q_ref, k_ref, v_ref, g_ref, dec_ref, qd_ref, kd_ref, gl_ref,
                       o_ref, st_ref, s_scr):
    c = pl.program_id(1)

    @pl.when(c == 0)
    def _():
        s_scr[...] = jnp.zeros_like(s_scr)

    for hh in range(R_HEADS):
        ks = slice(hh * R_DK, (hh + 1) * R_DK)
        vs = slice(hh * R_DV, (hh + 1) * R_DV)
        s = s_scr[hh]
        for j in range(RET_CHUNKS_PER_STEP):
            rows = slice(j * R_CHUNK, (j + 1) * R_CHUNK)
            o, s = _retention_head(q_ref[rows, ks], k_ref[rows, ks], v_ref[rows, vs], s,
                                   dec_ref[hh], qd_ref[hh], kd_ref[hh], gl_ref[hh])
            o_ref[rows, vs] = _head_rms_gate(o, g_ref[rows, vs]).astype(o_ref.dtype)
        s_scr[hh] = s

    @pl.when(c == pl.num_programs(1) - 1)
    def _():
        st_ref[...] = s_scr[...]


def _retention_prompt(rq, rk, rv, rg, tabs, B, S):
    rows = RET_CHUNKS_PER_STEP * R_CHUNK
    nc = S // rows
    row = lambda b, c: (b * nc + c, 0)
    dec, qd, kd, gl = tabs
    return pl.pallas_call(
        _ret_prompt_kernel,
        grid=(B, nc),
        in_specs=[pl.BlockSpec((rows, R_QK), row), pl.BlockSpec((rows, R_QK), row),
                  pl.BlockSpec((rows, R_V), row), pl.BlockSpec((rows, R_V), row),
                  _resident(dec.shape), _resident(qd.shape), _resident(kd.shape),
                  _resident(gl.shape)],
        out_specs=[pl.BlockSpec((rows, R_V), row),
                   pl.BlockSpec((None, R_HEADS, R_DK, R_DV), lambda b, c: (b, 0, 0, 0))],
        out_shape=[jax.ShapeDtypeStruct((B * S, R_V), BF16),
                   jax.ShapeDtypeStruct((B, R_HEADS, R_DK, R_DV), F32)],
        scratch_shapes=[pltpu.VMEM((R_HEADS, R_DK, R_DV), F32)],
        compiler_params=_params(2),
        name="retention_prompt",
    )(rq, rk, rv, rg, dec, qd, kd, gl)


def _ret_sample_kernel(q_ref, k_ref, v_ref, g_ref, s_ref, dec_ref, qd_ref, kd_ref, gl_ref,
                       o_ref, st_ref):
    for hh in range(R_HEADS):
        ks = slice(hh * R_DK, (hh + 1) * R_DK)
        vs = slice(hh * R_DV, (hh + 1) * R_DV)
        o, s_new = _retention_head(q_ref[:, ks], k_ref[:, ks], v_ref[:, vs], s_ref[hh],
                                   dec_ref[hh], qd_ref[hh], kd_ref[hh], gl_ref[hh])
        st_ref[hh] = s_new
        o_ref[:, vs] = _head_rms_gate(o, g_ref[:, vs]).astype(o_ref.dtype)


def _retention_sample(rq, rk, rv, rg, state_all, layer, tabs):
    DB, LP = rq.shape[:2]
    dec, qd, kd, gl = tabs
    tok = lambda w: pl.BlockSpec((None, LP, w), lambda b: (b, 0, 0))
    st_in = pl.BlockSpec((None, None, R_HEADS, R_DK, R_DV), lambda b: (layer, b, 0, 0, 0))
    return pl.pallas_call(
        _ret_sample_kernel,
        grid=(DB,),
        in_specs=[tok(R_QK), tok(R_QK), tok(R_V), tok(R_V), st_in,
                  _resident(dec.shape), _resident(qd.shape), _resident(kd.shape),
                  _resident(gl.shape)],
        out_specs=[tok(R_V),
                   pl.BlockSpec((None, R_HEADS, R_DK, R_DV), lambda b: (b, 0, 0, 0))],
        out_shape=[jax.ShapeDtypeStruct((DB, LP, R_V), F32),
                   jax.ShapeDtypeStruct((DB, R_HEADS, R_DK, R_DV), F32)],
        compiler_params=_params(1),
        name="retention_sample",
    )(rq, rk, rv, rg, state_all, dec, qd, kd, gl)


def _retention_tables(L, l_real):
    log_gamma = jnp.log1p(-jnp.exp2(-5.0 - jnp.arange(R_HEADS, dtype=F32)))
    n = jnp.arange(L, dtype=F32)
    diff = n[:, None] - n[None, :]
    dec = jnp.where(diff >= 0, jnp.exp(log_gamma[:, None, None] * jnp.maximum(diff, 0.0)), 0.0)
    q_decay = jnp.exp(log_gamma[:, None] * (n[None, :] + 1.0))
    k_decay = jnp.exp(log_gamma[:, None] * (l_real - 1.0 - n[None, :]))
    qd = jnp.broadcast_to(q_decay[:, :, None], (R_HEADS, L, R_DK))
    kd = jnp.broadcast_to(k_decay[:, :, None], (R_HEADS, L, R_DK))
    gl = jnp.broadcast_to(jnp.exp(log_gamma * l_real)[:, None, None], (R_HEADS, 1, R_DV))
    return dec, qd, kd, gl


def _rank_select(scores, n_idx, n_cand, axis):
    nb = scores.shape[axis]
    cnt = jnp.zeros(scores.shape, F32)
    for m in range(nb):
        sm = scores[m:m + 1, :] if axis == 0 else scores[:, m:m + 1]
        beats = (sm > scores) | ((sm == scores) & (m < n_idx))
        cnt = cnt + jnp.where(beats, jnp.where(m < n_cand, 1.0, 0.0), 0.0)
    return jnp.where((n_idx < n_cand) & (cnt < float(M_TOPK)), 1.0, 0.0)


BF16_ROWS = 16
V_ROWS = M_DH + BF16_ROWS
LOG2E = 1.4426950408889634
CHUNK_BLOCKS = 4


def _moba_prompt_kernel(qt_ref, kt_ref, vt_ref, o_ref, kaug_scr, vaug_scr, vblk_scr, means_scr,
                        s_scr, sd_scr, m_scr, acc_scr):
    i = pl.program_id(2)
    nb = means_scr.shape[0]
    tq = qt_ref.shape[1]
    ck = CHUNK_BLOCKS * M_BLOCK
    own_rows = (slice(0, M_DH), slice(M_DH, LANES))
    onehot_lane0 = (M_DH, 0)

    @pl.when(i == 0)
    def _():
        lane = lax.broadcasted_iota(jnp.int32, (M_BLOCK, LANES), 1)
        ones_tile = jnp.ones((BF16_ROWS, M_BLOCK), BF16)
        for n in range(nb):
            cols = slice(n * M_BLOCK, (n + 1) * M_BLOCK)
            kblk = kt_ref[:, cols].T
            means_scr[n:n + 1, :] = jnp.sum(kblk, axis=0, keepdims=True) * (1.0 / M_BLOCK)
            for hd in range(2):
                mine = (lane < M_DH) if hd == 0 else (lane >= M_DH)
                other = jnp.where(lane == onehot_lane0[hd] + n, 1.0, 0.0)
                kaug_scr[hd, cols, :] = jnp.where(mine, kblk, other).astype(BF16)
                vb = vt_ref[own_rows[hd], cols].astype(BF16)
                vaug_scr[hd, 0:M_DH, cols] = vb
                vaug_scr[hd, M_DH:V_ROWS, cols] = ones_tile
                vblk_scr[hd, n, 0:M_DH, :] = vb
                vblk_scr[hd, n, M_DH:V_ROWS, :] = ones_tile

    qt = qt_ref[...]
    row = lax.broadcasted_iota(jnp.int32, qt.shape, 0)
    n_idx = lax.broadcasted_iota(jnp.int32, (nb, tq), 0)
    means = means_scr[...]
    causal = (lax.broadcasted_iota(jnp.int32, (M_BLOCK, tq), 0)
              <= lax.broadcasted_iota(jnp.int32, (M_BLOCK, tq), 1))
    own_rows_k = pl.ds(pl.multiple_of(i * M_BLOCK, M_BLOCK), M_BLOCK)
    pad = jnp.zeros((M_DH - nb, tq), BF16)
    zeros_half = jnp.zeros((M_DH, tq), BF16)
    rhs = []
    for hd in range(2):
        mine = (row < M_DH) if hd == 0 else (row >= M_DH)
        qh = jnp.where(mine, qt, 0.0)
        sel = _rank_select(_dot(means, qh, precision=HIGHEST), n_idx, i, axis=0)
        bias = jnp.where(sel > 0.5, 0.0, NEG).astype(BF16)
        q_own = (qt[own_rows[hd]] * (M_DH ** -0.5 * LOG2E)).astype(BF16)
        parts = [q_own, bias, pad]
        rhs.append(jnp.concatenate(parts if hd == 0 else parts[1:] + parts[:1], axis=0))
        q_only = jnp.concatenate([q_own, zeros_half] if hd == 0 else [zeros_half, q_own], axis=0)
        s_d = jnp.where(causal, _dot(kaug_scr[hd, own_rows_k, :], q_only), NEG)
        sd_scr[hd] = s_d
        m_scr[hd] = jnp.max(s_d, axis=0, keepdims=True)

    for c in range(nb // CHUNK_BLOCKS):
        @pl.when(c * CHUNK_BLOCKS < i)
        def _():
            for hd in range(2):
                s = _dot(kaug_scr[hd, c * ck:(c + 1) * ck, :], rhs[hd])
                s_scr[hd, c] = s
                m_scr[hd] = jnp.maximum(m_scr[hd], jnp.max(s, axis=0, keepdims=True))

    for hd in range(2):
        p_d = jnp.exp2(sd_scr[hd] - m_scr[hd]).astype(BF16)
        acc_scr[hd] = _dot(vblk_scr[hd, i], p_d)

    for c in range(nb // CHUNK_BLOCKS):
        @pl.when(c * CHUNK_BLOCKS < i)
        def _():
            for hd in range(2):
                p = jnp.exp2(s_scr[hd, c] - m_scr[hd]).astype(BF16)
                acc_scr[hd] = acc_scr[hd] + _dot(vaug_scr[hd, :, c * ck:(c + 1) * ck], p)

    o_t = jnp.concatenate([acc_scr[0, 0:M_DH, :] / acc_scr[0, M_DH:M_DH + 1, :],
                           acc_scr[1, 0:M_DH, :] / acc_scr[1, M_DH:M_DH + 1, :]], axis=0)
    o_ref[...] = o_t.T.astype(o_ref.dtype)


def _moba_prompt(mqt, mkt, mvt):
    B, _, S = mqt.shape
    nb = S // M_BLOCK
    tq = M_BLOCK
    npair = M_W // LANES
    assert nb % CHUNK_BLOCKS == 0 and nb <= M_DH and nb % BF16_ROWS == 0
    return pl.pallas_call(
        _moba_prompt_kernel,
        grid=(B, npair, nb),
        in_specs=[pl.BlockSpec((None, LANES, tq), lambda b, p, i: (b, p, i)),
                  pl.BlockSpec((None, LANES, S), lambda b, p, i: (b, p, 0)),
                  pl.BlockSpec((None, LANES, S), lambda b, p, i: (b, p, 0))],
        out_specs=pl.BlockSpec((tq, LANES), lambda b, p, i: (b * nb + i, p)),
        out_shape=jax.ShapeDtypeStruct((B * S, M_W), BF16),
        scratch_shapes=[pltpu.VMEM((2, S, LANES), BF16),
                        pltpu.VMEM((2, V_ROWS, S), BF16),
                        pltpu.VMEM((2, nb, V_ROWS, M_BLOCK), BF16),
                        pltpu.VMEM((nb, LANES), F32),
                        pltpu.VMEM((2, nb // CHUNK_BLOCKS, CHUNK_BLOCKS * M_BLOCK, tq), F32),
                        pltpu.VMEM((2, M_BLOCK, tq), F32),
                        pltpu.VMEM((2, 1, tq), F32),
                        pltpu.VMEM((2, V_ROWS, tq), F32)],
        compiler_params=_params(3),
        name="moba_prompt",
    )(mqt, mkt, mvt)


PAGES_PER_STEP = 32


def _own_head(shape):
    row = lax.broadcasted_iota(jnp.int32, shape, 0)
    lane = lax.broadcasted_iota(jnp.int32, shape, 1)
    return (lane // M_DH) == (row % M_HEADS)


def _moba_keys_kernel(pt_ref, q_ref, *refs):
    k_refs = refs[:PAGES_PER_STEP]
    s_ref, bsum_ref = refs[PAGES_PER_STEP:]
    step = pl.program_id(1)

    @pl.when(step == 0)
    def _():
        bsum_ref[...] = jnp.zeros_like(bsum_ref)

    q = q_ref[...]
    qb = (jnp.where(_own_head(q.shape), q, 0.0) * (M_DH ** -0.5)).astype(BF16)
    ppb = M_BLOCK // PAGE_SIZE
    blk_lane = lax.broadcasted_iota(jnp.int32, bsum_ref.shape, 1)
    bsum = bsum_ref[...]
    for j in range(PAGES_PER_STEP):
        kpt = k_refs[j][...].reshape(M_W, PAGE_SIZE)
        s_ref[:, j * PAGE_SIZE:(j + 1) * PAGE_SIZE] = _dot(qb, kpt.astype(BF16))
        blk_acc = kpt if j % ppb == 0 else blk_acc + kpt
        if j % ppb == ppb - 1:
            rowsum = jnp.sum(blk_acc, axis=1, keepdims=True)
            n = step * (PAGES_PER_STEP // ppb) + j // ppb
            bsum = bsum + jnp.where(blk_lane == n, rowsum, 0.0)
    bsum_ref[...] = bsum


def _page_specs(layer):
    def spec(j):
        return pl.BlockSpec(
            (None, None, M_HEADS, M_DH, PAGE_SIZE),
            lambda b, s, pt: (layer, pt[b, s * PAGES_PER_STEP + j], 0, 0, 0))
    return [spec(j) for j in range(PAGES_PER_STEP)]


def _moba_keys(page_table, q_rep, cache_kt, layer):
    DB, n_pages = page_table.shape
    R = q_rep.shape[1]
    steps = n_pages // PAGES_PER_STEP
    nb = n_pages * PAGE_SIZE // M_BLOCK
    grid_spec = pltpu.PrefetchScalarGridSpec(
        num_scalar_prefetch=1,
        grid=(DB, steps),
        in_specs=[pl.BlockSpec((None, R, M_W), lambda b, s, pt: (b, 0, 0))] + _page_specs(layer),
        out_specs=[pl.BlockSpec((None, R, PAGES_PER_STEP * PAGE_SIZE), lambda b, s, pt: (b, 0, s)),
                   pl.BlockSpec((None, M_W, nb), lambda b, s, pt: (b, 0, 0))],
    )
    return pl.pallas_call(
        _moba_keys_kernel,
        grid_spec=grid_spec,
        out_shape=[jax.ShapeDtypeStruct((DB, R, n_pages * PAGE_SIZE), F32),
                   jax.ShapeDtypeStruct((DB, M_W, nb), F32)],
        compiler_params=_params(2),
        name="moba_sample_keys",
    )(page_table, q_rep, *([cache_kt] * PAGES_PER_STEP))


def _moba_probs_kernel(s_ref, bsum_ref, q_ref, kn_ref, vn_ref, p_ref, own_ref):
    q = q_ref[...]
    R = q.shape[0]
    qh = jnp.where(_own_head(q.shape), q, 0.0)
    means_t = bsum_ref[...] * (1.0 / M_BLOCK)
    nb = means_t.shape[1]
    sc = _dot(qh, means_t, precision=HIGHEST)
    n_idx = lax.broadcasted_iota(jnp.int32, (R, nb), 1)
    sel = _rank_select(sc, n_idx, nb, axis=1)
    s = s_ref[...]
    past = s.shape[1]
    blk_of_key = lax.broadcasted_iota(jnp.int32, (nb, past), 1) // M_BLOCK
    expand = jnp.where(blk_of_key == lax.broadcasted_iota(jnp.int32, (nb, past), 0), 1.0, 0.0)
    keep = _dot(sel.astype(BF16), expand.astype(BF16)) > 0.5
    s = jnp.where(keep, s, NEG)
    qb = (qh * (M_DH ** -0.5)).astype(BF16)
    s_own = _dot_nt(qb, kn_ref[...].astype(BF16))
    tok = lax.broadcasted_iota(jnp.int32, s_own.shape, 0) // M_HEADS
    s_own = jnp.where(lax.broadcasted_iota(jnp.int32, s_own.shape, 1) <= tok, s_own, NEG)
    m = jnp.maximum(jnp.max(s, axis=1, keepdims=True), jnp.max(s_own, axis=1, keepdims=True))
    p = jnp.exp(s - m)
    p_own = jnp.exp(s_own - m)
    inv = 1.0 / (jnp.sum(p, axis=1, keepdims=True) + jnp.sum(p_own, axis=1, keepdims=True))
    p_ref[...] = (p * inv).astype(p_ref.dtype)
    own_ref[...] = _dot((p_own * inv).astype(BF16), vn_ref[...].astype(BF16))


def _moba_probs(scores, bsum_t, q_rep, k_new, v_new):
    DB, R, past = scores.shape
    nb = bsum_t.shape[2]
    LP = k_new.shape[1]
    per_b = lambda *shape: pl.BlockSpec((None,) + shape, lambda b: (b, 0, 0))
    return pl.pallas_call(
        _moba_probs_kernel,
        grid=(DB,),
        in_specs=[per_b(R, past), per_b(M_W, nb), per_b(R, M_W), per_b(LP, M_W), per_b(LP, M_W)],
        out_specs=[per_b(R, past), per_b(R, M_W)],
        out_shape=[jax.ShapeDtypeStruct((DB, R, past), BF16),
                   jax.ShapeDtypeStruct((DB, R, M_W), F32)],
        compiler_params=_params(1),
        name="moba_sample_probs",
    )(scores, bsum_t, q_rep, k_new, v_new)


def _moba_values_kernel(pt_ref, p_ref, own_ref, *refs):
    v_refs = refs[:PAGES_PER_STEP]
    o_ref, acc_scr = refs[PAGES_PER_STEP:]
    step = pl.program_id(1)

    @pl.when(step == 0)
    def _():
        acc_scr[...] = own_ref[...]

    acc = acc_scr[...]
    for j in range(PAGES_PER_STEP):
        vpt = v_refs[j][...].reshape(M_W, PAGE_SIZE).astype(BF16)
        acc = acc + _dot_nt(p_ref[:, j * PAGE_SIZE:(j + 1) * PAGE_SIZE], vpt)
    acc_scr[...] = acc

    @pl.when(step == pl.num_programs(1) - 1)
    def _():
        R = acc.shape[0]
        picked = jnp.where(_own_head(acc.shape), acc, 0.0)
        o_ref[...] = jnp.sum(picked.reshape(R // M_HEADS, M_HEADS, M_W), axis=1).astype(o_ref.dtype)


def _moba_values(page_table, probs, own, cache_vt, layer):
    DB, n_pages = page_table.shape
    R = probs.shape[1]
    steps = n_pages // PAGES_PER_STEP
    grid_spec = pltpu.PrefetchScalarGridSpec(
        num_scalar_prefetch=1,
        grid=(DB, steps),
        in_specs=[pl.BlockSpec((None, R, PAGES_PER_STEP * PAGE_SIZE), lambda b, s, pt: (b, 0, s)),
                  pl.BlockSpec((None, R, M_W), lambda b, s, pt: (b, 0, 0))] + _page_specs(layer),
        out_specs=pl.BlockSpec((None, R // M_HEADS, M_W), lambda b, s, pt: (b, 0, 0)),
        scratch_shapes=[pltpu.VMEM((R, M_W), F32)],
    )
    return pl.pallas_call(
        _moba_values_kernel,
        grid_spec=grid_spec,
        out_shape=jax.ShapeDtypeStruct((DB, R // M_HEADS, M_W), F32),
        compiler_params=_params(2),
        name="moba_sample_values",
    )(page_table, probs, own, *([cache_vt] * PAGES_PER_STEP))


def _merge_kernel(x_ref, ro_ref, mo_ref, gr_ref, ga_ref, wr_ref, wa_ref, wo_ref, y_ref):
    merged = (jax.nn.sigmoid(gr_ref[...]) * _dot(ro_ref[...].astype(BF16), wr_ref[...])
              + jax.nn.sigmoid(ga_ref[...]) * _dot(mo_ref[...].astype(BF16), wa_ref[...]))
    y_ref[...] = x_ref[...] + _dot(merged.astype(BF16), wo_ref[...])


def _merge(x, ro, mo, gr, ga, wr, wa, wo, tm):
    T = x.shape[0]
    row = lambda i: (i, 0)
    blk = lambda w: pl.BlockSpec((tm, w), row)
    return pl.pallas_call(
        _merge_kernel,
        grid=(T // tm,),
        in_specs=[blk(D_MODEL), blk(R_V), blk(M_W), blk(D_MODEL), blk(D_MODEL),
                  _resident(wr.shape), _resident(wa.shape), _resident(wo.shape)],
        out_specs=blk(D_MODEL),
        out_shape=jax.ShapeDtypeStruct((T, D_MODEL), F32),
        compiler_params=_params(1),
        name="merge",
    )(x, ro, mo, gr, ga, wr, wa, wo)


FF_CHUNK = 256


def _ffn_kernel(x_ref, g_ref, wgu_ref, wd_ref, gf_ref, y_ref, h_scr, act_scr, *, final_norm):
    x = x_ref[...]
    h_scr[...] = _rms(x, g_ref[...]).astype(BF16)
    for c0 in range(0, D_FF, FF_CHUNK):
        gate = _dot(h_scr[...], wgu_ref[:, c0:c0 + FF_CHUNK])
        up = _dot(h_scr[...], wgu_ref[:, D_FF + c0:D_FF + c0 + FF_CHUNK])
        act_scr[:, c0:c0 + FF_CHUNK] = (gate * jax.nn.sigmoid(gate) * up).astype(BF16)
    y = x + _dot(act_scr[...], wd_ref[...])
    if final_norm:
        y = _rms(y, gf_ref[...])
    y_ref[...] = y


def _ffn(x, g, wgu, wd, g_final, tm, final_norm):
    T = x.shape[0]
    row = lambda i: (i, 0)
    return pl.pallas_call(
        functools.partial(_ffn_kernel, final_norm=final_norm),
        grid=(T // tm,),
        in_specs=[pl.BlockSpec((tm, D_MODEL), row), _resident((1, D_MODEL)),
                  _resident(wgu.shape), _resident(wd.shape), _resident((1, D_MODEL))],
        out_specs=pl.BlockSpec((tm, D_MODEL), row),
        out_shape=jax.ShapeDtypeStruct((T, D_MODEL), F32),
        scratch_shapes=[pltpu.VMEM((tm, D_MODEL), BF16), pltpu.VMEM((tm, D_FF), BF16)],
        compiler_params=_params(1),
        name="ffn",
    )(x, g, wgu, wd, g_final)


def _rotation_tables(pos):
    posf = pos.astype(F32)
    angle = 1.0 / (10000.0 ** jnp.linspace(0.0, 1.0, R_DK // 2, dtype=F32))
    ang = jnp.repeat(posf[:, None] * angle[None, :], 2, axis=-1)
    sign = jnp.where(jnp.arange(R_DK) % 2 == 0, -1.0, 1.0).astype(F32)
    half = M_DH // 2
    inv = ROPE_THETA ** (-jnp.arange(half, dtype=F32) / half)
    ang_m = (posf[:, None] * inv[None, :]).T
    return jnp.cos(ang), jnp.sin(ang) * sign[None, :], jnp.cos(ang_m), jnp.sin(ang_m)


def _layer_tail(x, ro, mo, gr, ga, wts, l, tm, final_norm):
    x = _merge(x, ro, mo, gr, ga, wts["w_br_ret"][l], wts["w_br_att"][l], wts["w_out"][l], tm)
    return _ffn(x, wts["g_ffn"][l], wts["w_gu"][l], wts["w_down"][l], wts["g_final"], tm, final_norm)


def _feature_major_rows(t, n_seq):
    depth, _, _, S = t.shape
    return t.reshape(depth, n_seq, M_HEADS, M_DH, S).transpose(0, 1, 4, 2, 3)


def _prompt_group(x_prompt, wts):
    B, S, _ = x_prompt.shape
    depth = wts["w_in"].shape[0]
    tm = 512
    x = x_prompt.reshape(B * S, D_MODEL)
    rot = _rotation_tables(jnp.arange(S))
    ret_tabs = _retention_tables(R_CHUNK, float(R_CHUNK))
    ks, vs, states = [], [], []
    for l in range(depth):
        rq, rk, rv, rg, gr, ga, mqt, mkt, mvt = _in_proj(
            x, wts["g_mix"][l], wts["w_in"][l], rot, B, tm, BF16)
        ro, st = _retention_prompt(rq, rk, rv, rg, ret_tabs, B, S)
        mo = _moba_prompt(mqt, mkt, mvt)
        x = _layer_tail(x, ro, mo, gr, ga, wts, l, tm, l == depth - 1)
        ks.append(mkt)
        vs.append(mvt)
        states.append(st)
    return (x.reshape(B, S, D_MODEL), _feature_major_rows(jnp.stack(ks), B),
            _feature_major_rows(jnp.stack(vs), B), jnp.stack(states))


def _sample_group(x_sample, cache_k, cache_v, state_ret, page_table, wts):
    DB, L, _ = x_sample.shape
    depth = wts["w_in"].shape[0]
    T = DB * L
    LP = 8
    past_len = page_table.shape[1] * PAGE_SIZE
    x = x_sample.reshape(T, D_MODEL)
    rot = _rotation_tables(past_len + (jnp.arange(T) % L))
    ret_tabs = _retention_tables(LP, float(L))
    ckt = cache_k.transpose(0, 1, 3, 4, 2)
    cvt = cache_v.transpose(0, 1, 3, 4, 2)

    def per_seq(t):
        return jnp.pad(t.reshape(DB, L, -1), ((0, 0), (0, LP - L), (0, 0)))

    ks, vs, states = [], [], []
    for l in range(depth):
        rq, rk, rv, rg, gr, ga, mqt, mkt, mvt = _in_proj(
            x, wts["g_mix"][l], wts["w_in"][l], rot, 1, T, F32)
        mq, mk, mv = mqt[0].T, mkt[0].T, mvt[0].T
        ro, st = _retention_sample(per_seq(rq), per_seq(rk), per_seq(rv), per_seq(rg),
                                   state_ret, l, ret_tabs)
        q_rep = jnp.repeat(mq.reshape(DB, L, M_W), M_HEADS, axis=1)
        scores, bsum_t = _moba_keys(page_table, q_rep, ckt, l)
        probs, own = _moba_probs(scores, bsum_t, q_rep, per_seq(mk), per_seq(mv))
        mo = _moba_values(page_table, probs, own, cvt, l)
        x = _layer_tail(x, ro[:, :L].reshape(T, R_V), mo.reshape(T, M_W), gr, ga, wts, l, T,
                        l == depth - 1)
        ks.append(mk.reshape(DB, L, M_HEADS, M_DH))
        vs.append(mv.reshape(DB, L, M_HEADS, M_DH))
        states.append(st)
    return x.reshape(DB, L, D_MODEL), jnp.stack(ks), jnp.stack(vs), jnp.stack(states)


def kernel(x_prompt, x_sample, cache_k, cache_v, state_ret, page_table, g_mix, w_in, w_br_ret,
           w_br_att, w_out, g_ffn, w_gu, w_down, g_final):
    depth = w_in.shape[0]
    wts = {
        "g_mix": g_mix.reshape(depth, 1, D_MODEL), "g_ffn": g_ffn.reshape(depth, 1, D_MODEL),
        "g_final": g_final.reshape(1, D_MODEL),
        "w_in": w_in.astype(BF16),
        "w_br_ret": w_br_ret.astype(BF16), "w_br_att": w_br_att.astype(BF16),
        "w_out": w_out.astype(BF16), "w_gu": w_gu.astype(BF16), "w_down": w_down.astype(BF16),
    }
    y_p, k_p, v_p, st_p = _prompt_group(x_prompt, wts)
    y_s, k_s, v_s, st_s = _sample_group(x_sample, cache_k, cache_v, state_ret, page_table, wts)
    return (y_p, y_s, k_p, v_p, st_p.astype(state_ret.dtype), k_s, v_s, st_s.astype(state_ret.dtype))
```

```python
import functools

import jax
import jax.numpy as jnp
from jax import lax
from jax.experimental import pallas as pl
from jax.experimental.pallas import tpu as pltpu

F32 = jnp.float32
BF16 = jnp.bfloat16

D_MODEL = 1024
R_HEADS = 4
R_DK = 128
R_DV = 256
R_CHUNK = 128
R_QK = R_HEADS * R_DK
R_V = R_HEADS * R_DV
M_HEADS = 8
M_DH = 64
M_W = M_HEADS * M_DH
M_BLOCK = 256
M_TOPK = 3
PAGE_SIZE = 128
ROPE_THETA = 10000.0
D_FF = 2816
NORM_EPS = 1e-6
NEG = -1e30

C_RQ, C_RK, C_RV, C_RG = 0, 512, 1024, 2048
C_MQ = 3072
C_GR, C_GA = 4608, 5632
IN_WIDTH = 6656

LANES = 128
VMEM_LIMIT = 56 * 1024 * 1024
HIGHEST = lax.Precision.HIGHEST

NT_DIMS = (((1,), (1,)), ((), ()))
TN_DIMS = (((0,), (0,)), ((), ()))


def _params(n_axes):
    return pltpu.CompilerParams(
        dimension_semantics=("arbitrary",) * n_axes, vmem_limit_bytes=VMEM_LIMIT)


def _dot(a, b, precision=None):
    return jnp.dot(a, b, precision=precision, preferred_element_type=F32)


def _dot_nt(a, b):
    return lax.dot_general(a, b, NT_DIMS, preferred_element_type=F32)


def _dot_tn(a, b):
    return lax.dot_general(a, b, TN_DIMS, preferred_element_type=F32)


def _resident(shape):
    nd = len(shape)
    return pl.BlockSpec(shape, lambda *_: (0,) * nd, pipeline_mode=pl.Buffered(1))


def _rms(x, g):
    return x * lax.rsqrt(jnp.mean(x * x, axis=-1, keepdims=True) + NORM_EPS) * g


def _in_proj_kernel(x_ref, g_ref, w_ref, cr_ref, sr_ref, cmt_ref, smt_ref, *refs, first_layer):
    if first_layer:
        rq_ref, rk_ref, rv_ref, rg_ref, gr_ref, ga_ref, mqt_ref, k_all, v_all, h_scr = refs
        mkt_ref, mvt_ref = k_all.at[0], v_all.at[0]
        for later in range(1, k_all.shape[0]):
            k_all[later] = jnp.zeros(k_all.shape[1:], k_all.dtype)
            v_all[later] = jnp.zeros(v_all.shape[1:], v_all.dtype)
    else:
        _, _, rq_ref, rk_ref, rv_ref, rg_ref, gr_ref, ga_ref, mqt_ref, mkt_ref, mvt_ref, h_scr = refs
    h_scr[...] = _rms(x_ref[...], g_ref[...]).astype(BF16)

    def seg(c0, width=512):
        return _dot(h_scr[...], w_ref[:, c0:c0 + width])

    tm = h_scr.shape[0]
    even_lane = (lax.broadcasted_iota(jnp.int32, (tm, LANES), 1) % 2) == 0
    cr, sr = cr_ref[...], sr_ref[...]

    for out_ref, c0, scale in ((rq_ref, C_RQ, None), (rk_ref, C_RK, R_DK ** -0.5)):
        a = seg(c0)
        for hh in range(R_HEADS):
            sl = slice(hh * LANES, (hh + 1) * LANES)
            ah = a[:, sl]
            partner = jnp.where(even_lane, pltpu.roll(ah, LANES - 1, 1), pltpu.roll(ah, 1, 1))
            r = ah * cr + partner * sr
            if scale is not None:
                r = r * scale
            out_ref[:, sl] = r.astype(out_ref.dtype)
    for j in range(2):
        cols = slice(j * 512, (j + 1) * 512)
        rv_ref[:, cols] = seg(C_RV + j * 512).astype(rv_ref.dtype)
        rg_ref[:, cols] = seg(C_RG + j * 512).astype(rg_ref.dtype)
        gr_ref[:, cols] = seg(C_GR + j * 512).astype(gr_ref.dtype)
        ga_ref[:, cols] = seg(C_GA + j * 512).astype(ga_ref.dtype)

    cmt, smt = cmt_ref[...], smt_ref[...]
    half = M_DH // 2
    for idx, out_ref in enumerate((mqt_ref, mkt_ref, mvt_ref)):
        at = seg(C_MQ + idx * M_W).T
        if idx == 2:
            out_ref[...] = at
            continue
        for hd in range(M_HEADS):
            x1 = at[hd * M_DH:hd * M_DH + half]
            x2 = at[hd * M_DH + half:(hd + 1) * M_DH]
            out_ref[hd * M_DH:hd * M_DH + half, :] = x1 * cmt - x2 * smt
            out_ref[hd * M_DH + half:(hd + 1) * M_DH, :] = x2 * cmt + x1 * smt


def _in_proj(x, g, w, tabs, n_seq, tm, r_dtype, layer, depth, kv_all):
    T = x.shape[0]
    S = T // n_seq
    per_seq = S // tm
    cr, sr, cmt, smt = tabs
    row = lambda i: (i, 0)
    tab_spec = pl.BlockSpec((tm, LANES), lambda i: (i % per_seq, 0))
    tabt_spec = pl.BlockSpec((M_DH // 2, tm), lambda i: (0, i % per_seq))
    widths = (R_QK, R_QK, R_V, R_V, D_MODEL, D_MODEL)
    dtypes = (r_dtype, r_dtype, r_dtype, F32, BF16, BF16)
    featmajor = pl.BlockSpec((None, M_W, tm), lambda i: (i // per_seq, 0, i % per_seq))
    in_specs = [pl.BlockSpec((tm, D_MODEL), row), _resident((1, D_MODEL)),
                _resident(w.shape), tab_spec, tab_spec, tabt_spec, tabt_spec]
    n_plain = len(widths) + 1
    if layer == 0:
        kv_spec = pl.BlockSpec((depth, None, M_W, tm), lambda i: (0, i // per_seq, 0, i % per_seq))
        extra, aliases = (), {}
    else:
        kv_spec = pl.BlockSpec((None, None, M_W, tm),
                               lambda i: (layer, i // per_seq, 0, i % per_seq))
        extra = tuple(kv_all)
        aliases = {len(in_specs): n_plain, len(in_specs) + 1: n_plain + 1}
        in_specs = in_specs + [pl.BlockSpec(memory_space=pl.ANY)] * 2
    return pl.pallas_call(
        functools.partial(_in_proj_kernel, first_layer=layer == 0),
        grid=(T // tm,),
        in_specs=in_specs,
        out_specs=[pl.BlockSpec((tm, wd), row) for wd in widths] + [featmajor, kv_spec, kv_spec],
        out_shape=[jax.ShapeDtypeStruct((T, wd), dt) for wd, dt in zip(widths, dtypes)]
        + [jax.ShapeDtypeStruct((n_seq, M_W, S), F32)]
        + [jax.ShapeDtypeStruct((depth, n_seq, M_W, S), F32)] * 2,
        scratch_shapes=[pltpu.VMEM((tm, D_MODEL), BF16)],
        input_output_aliases=aliases,
        compiler_params=_params(1),
        name="in_proj",
    )(x, g, w, cr, sr, cmt, smt, *extra)


def _head_rms_gate(o, gate):
    on = o * lax.rsqrt(jnp.mean(o * o, axis=-1, keepdims=True) + NORM_EPS)
    return on * (gate * jax.nn.sigmoid(gate))


def _retention_head(q, k, v, s_prev, dec, qd, kd, gl):
    qf, kf = q.astype(F32), k.astype(F32)
    vb = v.astype(BF16)
    inner = _dot_nt(q.astype(BF16), k.astype(BF16)) * dec
    o = _dot(inner.astype(BF16), vb) + _dot((qf * qd).astype(BF16), s_prev.astype(BF16))
    s_new = gl * s_prev + _dot_tn((kf * kd).astype(BF16), vb)
    return o, s_new


RET_CHUNKS_PER_STEP = 4


def _ret_prompt_kernel(q_ref, k_ref, v_ref, g_ref, dec_ref, qd_ref, kd_ref, gl_ref,
                       o_ref, st_ref, s_scr):
    c = pl.program_id(1)

    @pl.when(c == 0)
    def _():
        s_scr[...] = jnp.zeros_like(s_scr)

    for hh in range(R_HEADS):
        ks = slice(hh * R_DK, (hh + 1) * R_DK)
        vs = slice(hh * R_DV, (hh + 1) * R_DV)
        s = s_scr[hh]
        for j in range(RET_CHUNKS_PER_STEP):
            rows = slice(j * R_CHUNK, (j + 1) * R_CHUNK)
            o, s = _retention_head(q_ref[rows, ks], k_ref[rows, ks], v_ref[rows, vs], s,
                                   dec_ref[hh], qd_ref[hh], kd_ref[hh], gl_ref[hh])
            o_ref[rows, vs] = _head_rms_gate(o, g_ref[rows, vs]).astype(o_ref.dtype)
        s_scr[hh] = s

    @pl.when(c == pl.num_programs(1) - 1)
    def _():
        st_ref[...] = s_scr[...]


def _retention_prompt(rq, rk, rv, rg, tabs, B, S):
    rows = RET_CHUNKS_PER_STEP * R_CHUNK
    nc = S // rows
    row = lambda b, c: (b * nc + c, 0)
    dec, qd, kd, gl = tabs
    return pl.pallas_call(
        _ret_prompt_kernel,
        grid=(B, nc),
        in_specs=[pl.BlockSpec((rows, R_QK), row), pl.BlockSpec((rows, R_QK), row),
                  pl.BlockSpec((rows, R_V), row), pl.BlockSpec((rows, R_V), row),
                  _resident(dec.shape), _resident(qd.shape), _resident(kd.shape),
                  _resident(gl.shape)],
        out_specs=[pl.BlockSpec((rows, R_V), row),
                   pl.BlockSpec((None, R_HEADS, R_DK, R_DV), lambda b, c: (b, 0, 0, 0))],
        out_shape=[jax.ShapeDtypeStruct((B * S, R_V), BF16),
                   jax.ShapeDtypeStruct((B, R_HEADS, R_DK, R_DV), F32)],
        scratch_shapes=[pltpu.VMEM((R_HEADS, R_DK, R_DV), F32)],
        compiler_params=_params(2),
        name="retention_prompt",
    )(rq, rk, rv, rg, dec, qd, kd, gl)


def _ret_sample_kernel(q_ref, k_ref, v_ref, g_ref, s_ref, dec_ref, qd_ref, kd_ref, gl_ref,
                       o_ref, st_ref):
    for hh in range(R_HEADS):
        ks = slice(hh * R_DK, (hh + 1) * R_DK)
        vs = slice(hh * R_DV, (hh + 1) * R_DV)
        o, s_new = _retention_head(q_ref[:, ks], k_ref[:, ks], v_ref[:, vs], s_ref[hh],
                                   dec_ref[hh], qd_ref[hh], kd_ref[hh], gl_ref[hh])
        st_ref[hh] = s_new
        o_ref[:, vs] = _head_rms_gate(o, g_ref[:, vs]).astype(o_ref.dtype)


def _retention_sample(rq, rk, rv, rg, state_all, layer, tabs):
    DB, LP = rq.shape[:2]
    dec, qd, kd, gl = tabs
    tok = lambda w: pl.BlockSpec((None, LP, w), lambda b: (b, 0, 0))
    st_in = pl.BlockSpec((None, None, R_HEADS, R_DK, R_DV), lambda b: (layer, b, 0, 0, 0))
    return pl.pallas_call(
        _ret_sample_kernel,
        grid=(DB,),
        in_specs=[tok(R_QK), tok(R_QK), tok(R_V), tok(R_V), st_in,
                  _resident(dec.shape), _resident(qd.shape), _resident(kd.shape),
                  _resident(gl.shape)],
        out_specs=[tok(R_V),
                   pl.BlockSpec((None, R_HEADS, R_DK, R_DV), lambda b: (b, 0, 0, 0))],
        out_shape=[jax.ShapeDtypeStruct((DB, LP, R_V), F32),
                   jax.ShapeDtypeStruct((DB, R_HEADS, R_DK, R_DV), F32)],
        compiler_params=_params(1),
        name="retention_sample",
    )(rq, rk, rv, rg, state_all, dec, qd, kd, gl)


def _retention_tables(L, l_real):
    log_gamma = jnp.log1p(-jnp.exp2(-5.0 - jnp.arange(R_HEADS, dtype=F32)))
    n = jnp.arange(L, dtype=F32)
    diff = n[:, None] - n[None, :]
    dec = jnp.where(diff >= 0, jnp.exp(log_gamma[:, None, None] * jnp.maximum(diff, 0.0)), 0.0)
    q_decay = jnp.exp(log_gamma[:, None] * (n[None, :] + 1.0))
    k_decay = jnp.exp(log_gamma[:, None] * (l_real - 1.0 - n[None, :]))
    qd = jnp.broadcast_to(q_decay[:, :, None], (R_HEADS, L, R_DK))
    kd = jnp.broadcast_to(k_decay[:, :, None], (R_HEADS, L, R_DK))
    gl = jnp.broadcast_to(jnp.exp(log_gamma * l_real)[:, None, None], (R_HEADS, 1, R_DV))
    return dec, qd, kd, gl


def _rank_select(scores, n_idx, n_cand, axis):
    nb = scores.shape[axis]
    cnt = jnp.zeros(scores.shape, F32)
    for m in range(nb):
        sm = scores[m:m + 1, :] if axis == 0 else scores[:, m:m + 1]
        beats = (sm > scores) | ((sm == scores) & (m < n_idx))
        cnt = cnt + jnp.where(beats, jnp.where(m < n_cand, 1.0, 0.0), 0.0)
    return jnp.where((n_idx < n_cand) & (cnt < float(M_TOPK)), 1.0, 0.0)


BF16_ROWS = 16
V_ROWS = M_DH + BF16_ROWS
LOG2E = 1.4426950408889634
CHUNK_BLOCKS = 4


def _moba_prompt_kernel(qt_ref, kt_ref, vt_ref, o_ref, kaug_scr, vaug_scr, means_scr,
                        s_scr, m_scr, acc_scr):
    i = pl.program_id(2)
    nb = means_scr.shape[0]
    tq = qt_ref.shape[1]
    ck = CHUNK_BLOCKS * M_BLOCK
    own_rows = (slice(0, M_DH), slice(M_DH, LANES))
    onehot_lane0 = (M_DH, 0)

    @pl.when(i == 0)
    def _():
        lane = lax.broadcasted_iota(jnp.int32, (M_BLOCK, LANES), 1)
        ones_tile = jnp.ones((BF16_ROWS, M_BLOCK), BF16)
        for n in range(nb):
            cols = slice(n * M_BLOCK, (n + 1) * M_BLOCK)
            kblk = kt_ref[:, cols].T
            means_scr[n:n + 1, :] = jnp.sum(kblk, axis=0, keepdims=True) * (1.0 / M_BLOCK)
            for hd in range(2):
                mine = (lane < M_DH) if hd == 0 else (lane >= M_DH)
                other = jnp.where(lane == onehot_lane0[hd] + n, 1.0, 0.0)
                kaug_scr[hd, cols, :] = jnp.where(mine, kblk, other).astype(BF16)
                vaug_scr[hd, 0:M_DH, cols] = vt_ref[own_rows[hd], cols].astype(BF16)
                vaug_scr[hd, M_DH:V_ROWS, cols] = ones_tile

    qt = qt_ref[...]
    row = lax.broadcasted_iota(jnp.int32, qt.shape, 0)
    n_idx = lax.broadcasted_iota(jnp.int32, (nb, tq), 0)
    means = means_scr[...]
    pad = jnp.zeros((M_DH - nb, tq), BF16)
    rhs = []
    for hd in range(2):
        mine = (row < M_DH) if hd == 0 else (row >= M_DH)
        qh = jnp.where(mine, qt, 0.0)
        sel = _rank_select(_dot(means, qh, precision=HIGHEST), n_idx, i, axis=0)
        bias = jnp.where((sel > 0.5) | (n_idx == i), 0.0, NEG).astype(BF16)
        q_own = (qt[own_rows[hd]] * (M_DH ** -0.5 * LOG2E)).astype(BF16)
        parts = [q_own, bias, pad]
        rhs.append(jnp.concatenate(parts if hd == 0 else parts[1:] + parts[:1], axis=0))

    def chunk_scores(c, hd, with_causal):
        s = _dot(kaug_scr[hd, c * ck:(c + 1) * ck, :], rhs[hd])
        if with_causal:
            key = lax.broadcasted_iota(jnp.int32, (ck, tq), 0)
            qry = lax.broadcasted_iota(jnp.int32, (1, tq), 1) + (i - c * CHUNK_BLOCKS) * M_BLOCK
            s = jnp.where(key <= qry, s, NEG)
        s_scr[hd, c] = s
        m = jnp.max(s, axis=0, keepdims=True)
        m_scr[hd] = m if c == 0 else jnp.maximum(m_scr[hd], m)

    for c in range(nb // CHUNK_BLOCKS):
        @pl.when((c + 1) * CHUNK_BLOCKS <= i)
        def _():
            for hd in range(2):
                chunk_scores(c, hd, False)

        @pl.when((c * CHUNK_BLOCKS <= i) & (i < (c + 1) * CHUNK_BLOCKS))
        def _():
            for hd in range(2):
                chunk_scores(c, hd, True)

    for c in range(nb // CHUNK_BLOCKS):
        @pl.when(c * CHUNK_BLOCKS <= i)
        def _():
            for hd in range(2):
                p = jnp.exp2(s_scr[hd, c] - m_scr[hd]).astype(BF16)
                pv = _dot(vaug_scr[hd, :, c * ck:(c + 1) * ck], p)
                acc_scr[hd] = pv if c == 0 else acc_scr[hd] + pv

    o_t = jnp.concatenate([acc_scr[0, 0:M_DH, :] / acc_scr[0, M_DH:M_DH + 1, :],
                           acc_scr[1, 0:M_DH, :] / acc_scr[1, M_DH:M_DH + 1, :]], axis=0)
    o_ref[...] = o_t.T.astype(o_ref.dtype)


def _moba_prompt(mqt, k_all, v_all, layer):
    B, _, S = mqt.shape
    nb = S // M_BLOCK
    tq = M_BLOCK
    npair = M_W // LANES
    assert nb % CHUNK_BLOCKS == 0 and nb <= M_DH and nb % BF16_ROWS == 0
    return pl.pallas_call(
        _moba_prompt_kernel,
        grid=(B, npair, nb),
        in_specs=[pl.BlockSpec((None, LANES, tq), lambda b, p, i: (b, p, i)),
                  pl.BlockSpec((None, None, LANES, S), lambda b, p, i: (layer, b, p, 0)),
                  pl.BlockSpec((None, None, LANES, S), lambda b, p, i: (layer, b, p, 0))],
        out_specs=pl.BlockSpec((tq, LANES), lambda b, p, i: (b * nb + i, p)),
        out_shape=jax.ShapeDtypeStruct((B * S, M_W), BF16),
        scratch_shapes=[pltpu.VMEM((2, S, LANES), BF16),
                        pltpu.VMEM((2, V_ROWS, S), BF16),
                        pltpu.VMEM((nb, LANES), F32),
                        pltpu.VMEM((2, nb // CHUNK_BLOCKS, CHUNK_BLOCKS * M_BLOCK, tq), F32),
                        pltpu.VMEM((2, 1, tq), F32),
                        pltpu.VMEM((2, V_ROWS, tq), F32)],
        compiler_params=_params(3),
        name="moba_prompt",
    )(mqt, k_all, v_all)


PAGES_PER_STEP = 64


def _own_head(shape):
    row = lax.broadcasted_iota(jnp.int32, shape, 0)
    lane = lax.broadcasted_iota(jnp.int32, shape, 1)
    return (lane // M_DH) == (row % M_HEADS)


def _moba_keys_kernel(pt_ref, q_ref, *refs):
    k_refs = refs[:PAGES_PER_STEP]
    s_ref, bsum_ref = refs[PAGES_PER_STEP:]
    step = pl.program_id(1)

    @pl.when(step == 0)
    def _():
        bsum_ref[...] = jnp.zeros_like(bsum_ref)

    q = q_ref[...]
    qb = (jnp.where(_own_head(q.shape), q, 0.0) * (M_DH ** -0.5)).astype(BF16)
    ppb = M_BLOCK // PAGE_SIZE
    blk_lane = lax.broadcasted_iota(jnp.int32, bsum_ref.shape, 1)
    bsum = bsum_ref[...]
    for j in range(PAGES_PER_STEP):
        kpt = k_refs[j][...].reshape(M_W, PAGE_SIZE)
        s_ref[:, j * PAGE_SIZE:(j + 1) * PAGE_SIZE] = _dot(qb, kpt.astype(BF16))
        blk_acc = kpt if j % ppb == 0 else blk_acc + kpt
        if j % ppb == ppb - 1:
            rowsum = jnp.sum(blk_acc, axis=1, keepdims=True)
            n = step * (PAGES_PER_STEP // ppb) + j // ppb
            bsum = bsum + jnp.where(blk_lane == n, rowsum, 0.0)
    bsum_ref[...] = bsum


def _page_specs(layer):
    def spec(j):
        return pl.BlockSpec(
            (None, None, M_HEADS, M_DH, PAGE_SIZE),
            lambda b, s, pt: (layer, pt[b, s * PAGES_PER_STEP + j], 0, 0, 0))
    return [spec(j) for j in range(PAGES_PER_STEP)]


def _moba_keys(page_table, q_rep, cache_kt, layer):
    DB, n_pages = page_table.shape
    R = q_rep.shape[1]
    steps = n_pages // PAGES_PER_STEP
    nb = n_pages * PAGE_SIZE // M_BLOCK
    grid_spec = pltpu.PrefetchScalarGridSpec(
        num_scalar_prefetch=1,
        grid=(DB, steps),
        in_specs=[pl.BlockSpec((None, R, M_W), lambda b, s, pt: (b, 0, 0))] + _page_specs(layer),
        out_specs=[pl.BlockSpec((None, R, PAGES_PER_STEP * PAGE_SIZE), lambda b, s, pt: (b, 0, s)),
                   pl.BlockSpec((None, M_W, nb), lambda b, s, pt: (b, 0, 0))],
    )
    return pl.pallas_call(
        _moba_keys_kernel,
        grid_spec=grid_spec,
        out_shape=[jax.ShapeDtypeStruct((DB, R, n_pages * PAGE_SIZE), F32),
                   jax.ShapeDtypeStruct((DB, M_W, nb), F32)],
        compiler_params=_params(2),
        name="moba_sample_keys",
    )(page_table, q_rep, *([cache_kt] * PAGES_PER_STEP))


def _moba_probs_kernel(s_ref, bsum_ref, q_ref, kn_ref, vn_ref, p_ref, own_ref):
    q = q_ref[...]
    R = q.shape[0]
    qh = jnp.where(_own_head(q.shape), q, 0.0)
    means_t = bsum_ref[...] * (1.0 / M_BLOCK)
    nb = means_t.shape[1]
    sc = _dot(qh, means_t, precision=HIGHEST)
    n_idx = lax.broadcasted_iota(jnp.int32, (R, nb), 1)
    sel = _rank_select(sc, n_idx, nb, axis=1)
    s = s_ref[...]
    past = s.shape[1]
    blk_of_key = lax.broadcasted_iota(jnp.int32, (nb, past), 1) // M_BLOCK
    expand = jnp.where(blk_of_key == lax.broadcasted_iota(jnp.int32, (nb, past), 0), 1.0, 0.0)
    keep = _dot(sel.astype(BF16), expand.astype(BF16)) > 0.5
    s = jnp.where(keep, s, NEG)
    qb = (qh * (M_DH ** -0.5)).astype(BF16)
    s_own = _dot_nt(qb, kn_ref[...].astype(BF16))
    tok = lax.broadcasted_iota(jnp.int32, s_own.shape, 0) // M_HEADS
    s_own = jnp.where(lax.broadcasted_iota(jnp.int32, s_own.shape, 1) <= tok, s_own, NEG)
    m = jnp.maximum(jnp.max(s, axis=1, keepdims=True), jnp.max(s_own, axis=1, keepdims=True))
    p = jnp.exp(s - m)
    p_own = jnp.exp(s_own - m)
    inv = 1.0 / (jnp.sum(p, axis=1, keepdims=True) + jnp.sum(p_own, axis=1, keepdims=True))
    p_ref[...] = (p * inv).astype(p_ref.dtype)
    own_ref[...] = _dot((p_own * inv).astype(BF16), vn_ref[...].astype(BF16))


def _moba_probs(scores, bsum_t, q_rep, k_new, v_new):
    DB, R, past = scores.shape
    nb = bsum_t.shape[2]
    LP = k_new.shape[1]
    per_b = lambda *shape: pl.BlockSpec((None,) + shape, lambda b: (b, 0, 0))
    return pl.pallas_call(
        _moba_probs_kernel,
        grid=(DB,),
        in_specs=[per_b(R, past), per_b(M_W, nb), per_b(R, M_W), per_b(LP, M_W), per_b(LP, M_W)],
        out_specs=[per_b(R, past), per_b(R, M_W)],
        out_shape=[jax.ShapeDtypeStruct((DB, R, past), BF16),
                   jax.ShapeDtypeStruct((DB, R, M_W), F32)],
        compiler_params=_params(1),
        name="moba_sample_probs",
    )(scores, bsum_t, q_rep, k_new, v_new)


def _moba_values_kernel(pt_ref, p_ref, own_ref, *refs):
    v_refs = refs[:PAGES_PER_STEP]
    o_ref, acc_scr = refs[PAGES_PER_STEP:]
    step = pl.program_id(1)

    @pl.when(step == 0)
    def _():
        acc_scr[...] = own_ref[...]

    acc = acc_scr[...]
    for j in range(PAGES_PER_STEP):
        vpt = v_refs[j][...].reshape(M_W, PAGE_SIZE).astype(BF16)
        acc = acc + _dot_nt(p_ref[:, j * PAGE_SIZE:(j + 1) * PAGE_SIZE], vpt)
    acc_scr[...] = acc

    @pl.when(step == pl.num_programs(1) - 1)
    def _():
        R = acc.shape[0]
        picked = jnp.where(_own_head(acc.shape), acc, 0.0)
        o_ref[...] = jnp.sum(picked.reshape(R // M_HEADS, M_HEADS, M_W), axis=1).astype(o_ref.dtype)


def _moba_values(page_table, probs, own, cache_vt, layer):
    DB, n_pages = page_table.shape
    R = probs.shape[1]
    steps = n_pages // PAGES_PER_STEP
    grid_spec = pltpu.PrefetchScalarGridSpec(
        num_scalar_prefetch=1,
        grid=(DB, steps),
        in_specs=[pl.BlockSpec((None, R, PAGES_PER_STEP * PAGE_SIZE), lambda b, s, pt: (b, 0, s)),
                  pl.BlockSpec((None, R, M_W), lambda b, s, pt: (b, 0, 0))] + _page_specs(layer),
        out_specs=pl.BlockSpec((None, R // M_HEADS, M_W), lambda b, s, pt: (b, 0, 0)),
        scratch_shapes=[pltpu.VMEM((R, M_W), F32)],
    )
    return pl.pallas_call(
        _moba_values_kernel,
        grid_spec=grid_spec,
        out_shape=jax.ShapeDtypeStruct((DB, R // M_HEADS, M_W), F32),
        compiler_params=_params(2),
        name="moba_sample_values",
    )(page_table, probs, own, *([cache_vt] * PAGES_PER_STEP))


def _merge_kernel(x_ref, ro_ref, mo_ref, gr_ref, ga_ref, wr_ref, wa_ref, wo_ref, y_ref):
    merged = (jax.nn.sigmoid(gr_ref[...].astype(F32)) * _dot(ro_ref[...].astype(BF16), wr_ref[...])
              + jax.nn.sigmoid(ga_ref[...].astype(F32)) * _dot(mo_ref[...].astype(BF16), wa_ref[...]))
    y_ref[...] = x_ref[...] + _dot(merged.astype(BF16), wo_ref[...])


def _merge(x, ro, mo, gr, ga, wr, wa, wo, tm):
    T = x.shape[0]
    row = lambda i: (i, 0)
    blk = lambda w: pl.BlockSpec((tm, w), row)
    return pl.pallas_call(
        _merge_kernel,
        grid=(T // tm,),
        in_specs=[blk(D_MODEL), blk(R_V), blk(M_W), blk(D_MODEL), blk(D_MODEL),
                  _resident(wr.shape), _resident(wa.shape), _resident(wo.shape)],
        out_specs=blk(D_MODEL),
        out_shape=jax.ShapeDtypeStruct((T, D_MODEL), F32),
        compiler_params=_params(1),
        name="merge",
    )(x, ro, mo, gr, ga, wr, wa, wo)


FF_CHUNK = 256


def _ffn_kernel(x_ref, g_ref, wgu_ref, wd_ref, gf_ref, y_ref, h_scr, act_scr, *, final_norm):
    x = x_ref[...]
    h_scr[...] = _rms(x, g_ref[...]).astype(BF16)
    for c0 in range(0, D_FF, FF_CHUNK):
        gate = _dot(h_scr[...], wgu_ref[:, c0:c0 + FF_CHUNK])
        up = _dot(h_scr[...], wgu_ref[:, D_FF + c0:D_FF + c0 + FF_CHUNK])
        act_scr[:, c0:c0 + FF_CHUNK] = (gate * jax.nn.sigmoid(gate) * up).astype(BF16)
    y = x + _dot(act_scr[...], wd_ref[...])
    if final_norm:
        y = _rms(y, gf_ref[...])
    y_ref[...] = y


def _ffn(x, g, wgu, wd, g_final, tm, final_norm):
    T = x.shape[0]
    row = lambda i: (i, 0)
    return pl.pallas_call(
        functools.partial(_ffn_kernel, final_norm=final_norm),
        grid=(T // tm,),
        in_specs=[pl.BlockSpec((tm, D_MODEL), row), _resident((1, D_MODEL)),
                  _resident(wgu.shape), _resident(wd.shape), _resident((1, D_MODEL))],
        out_specs=pl.BlockSpec((tm, D_MODEL), row),
        out_shape=jax.ShapeDtypeStruct((T, D_MODEL), F32),
        scratch_shapes=[pltpu.VMEM((tm, D_MODEL), BF16), pltpu.VMEM((tm, D_FF), BF16)],
        compiler_params=_params(1),
        name="ffn",
    )(x, g, wgu, wd, g_final)


def _rotation_tables(pos):
    posf = pos.astype(F32)
    angle = 1.0 / (10000.0 ** jnp.linspace(0.0, 1.0, R_DK // 2, dtype=F32))
    ang = jnp.repeat(posf[:, None] * angle[None, :], 2, axis=-1)
    sign = jnp.where(jnp.arange(R_DK) % 2 == 0, -1.0, 1.0).astype(F32)
    half = M_DH // 2
    inv = ROPE_THETA ** (-jnp.arange(half, dtype=F32) / half)
    ang_m = (posf[:, None] * inv[None, :]).T
    return jnp.cos(ang), jnp.sin(ang) * sign[None, :], jnp.cos(ang_m), jnp.sin(ang_m)


def _layer_tail(x, ro, mo, gr, ga, wts, l, tm, final_norm):
    x = _merge(x, ro, mo, gr, ga, wts["w_br_ret"][l], wts["w_br_att"][l], wts["w_out"][l], tm)
    return _ffn(x, wts["g_ffn"][l], wts["w_gu"][l], wts["w_down"][l], wts["g_final"], tm, final_norm)


def _feature_major_rows(t, n_seq):
    depth, _, _, S = t.shape
    return t.reshape(depth, n_seq, M_HEADS, M_DH, S).transpose(0, 1, 4, 2, 3)


def _prompt_group(x_prompt, wts):
    B, S, _ = x_prompt.shape
    depth = wts["w_in"].shape[0]
    tm = 512
    x = x_prompt.reshape(B * S, D_MODEL)
    rot = _rotation_tables(jnp.arange(S))
    ret_tabs = _retention_tables(R_CHUNK, float(R_CHUNK))
    kv_all, states = None, []
    for l in range(depth):
        rq, rk, rv, rg, gr, ga, mqt, *kv_all = _in_proj(
            x, wts["g_mix"][l], wts["w_in"][l], rot, B, tm, BF16, l, depth, kv_all)
        ro, st = _retention_prompt(rq, rk, rv, rg, ret_tabs, B, S)
        mo = _moba_prompt(mqt, kv_all[0], kv_all[1], l)
        x = _layer_tail(x, ro, mo, gr, ga, wts, l, tm, l == depth - 1)
        states.append(st)
    return (x.reshape(B, S, D_MODEL), _feature_major_rows(kv_all[0], B),
            _feature_major_rows(kv_all[1], B), jnp.stack(states))


def _sample_group(x_sample, cache_k, cache_v, state_ret, page_table, wts):
    DB, L, _ = x_sample.shape
    depth = wts["w_in"].shape[0]
    T = DB * L
    LP = 8
    past_len = page_table.shape[1] * PAGE_SIZE
    x = x_sample.reshape(T, D_MODEL)
    rot = _rotation_tables(past_len + (jnp.arange(T) % L))
    ret_tabs = _retention_tables(LP, float(L))
    ckt = cache_k.transpose(0, 1, 3, 4, 2)
    cvt = cache_v.transpose(0, 1, 3, 4, 2)

    def per_seq(t):
        return jnp.pad(t.reshape(DB, L, -1), ((0, 0), (0, LP - L), (0, 0)))

    kv_all, states = None, []
    for l in range(depth):
        rq, rk, rv, rg, gr, ga, mqt, *kv_all = _in_proj(
            x, wts["g_mix"][l], wts["w_in"][l], rot, 1, T, F32, l, depth, kv_all)
        mq, mk, mv = mqt[0].T, kv_all[0][l, 0].T, kv_all[1][l, 0].T
        ro, st = _retention_sample(per_seq(rq), per_seq(rk), per_seq(rv), per_seq(rg),
                                   state_ret, l, ret_tabs)
        q_rep = jnp.repeat(mq.reshape(DB, L, M_W), M_HEADS, axis=1)
        scores, bsum_t = _moba_keys(page_table, q_rep, ckt, l)
        probs, own = _moba_probs(scores, bsum_t, q_rep, per_seq(mk), per_seq(mv))
        mo = _moba_values(page_table, probs, own, cvt, l)
        x = _layer_tail(x, ro[:, :L].reshape(T, R_V), mo.reshape(T, M_W), gr, ga, wts, l, T,
                        l == depth - 1)
        states.append(st)

    def new_rows(t):
        return t[:, 0].transpose(0, 2, 1).reshape(depth, DB, L, M_HEADS, M_DH)

    return x.reshape(DB, L, D_MODEL), new_rows(kv_all[0]), new_rows(kv_all[1]), jnp.stack(states)


def kernel(x_prompt, x_sample, cache_k, cache_v, state_ret, page_table, g_mix, w_in, w_br_ret,
           w_br_att, w_out, g_ffn, w_gu, w_down, g_final):
    depth = w_in.shape[0]
    wts = {
        "g_mix": g_mix.reshape(depth, 1, D_MODEL), "g_ffn": g_ffn.reshape(depth, 1, D_MODEL),
        "g_final": g_final.reshape(1, D_MODEL),
        "w_in": w_in.astype(BF16),
        "w_br_ret": w_br_ret.astype(BF16), "w_br_att": w_br_att.astype(BF16),
        "w_out": w_out.astype(BF16), "w_gu": w_gu.astype(BF16), "w_down": w_down.astype(BF16),
    }
    y_p, k_p, v_p, st_p = _prompt_group(x_prompt, wts)
    y_s, k_s, v_s, st_s = _sample_group(x_sample, cache_k, cache_v, state_ret, page_table, wts)
    return (y_p, y_s, k_p, v_p, st_p.astype(state_ret.dtype), k_s, v_s, st_s.astype(state_ret.dtype))
```

```python
import functools

import jax
import jax.numpy as jnp
from jax import lax
from jax.experimental import pallas as pl
from jax.experimental.pallas import tpu as pltpu

F32 = jnp.float32
BF16 = jnp.bfloat16

D_MODEL = 1024
R_HEADS = 4
R_DK = 128
R_DV = 256
R_CHUNK = 128
R_QK = R_HEADS * R_DK
R_V = R_HEADS * R_DV
M_HEADS = 8
M_DH = 64
M_W = M_HEADS * M_DH
M_BLOCK = 256
M_TOPK = 3
PAGE_SIZE = 128
ROPE_THETA = 10000.0
D_FF = 2816
NORM_EPS = 1e-6
NEG = -1e30

C_RQ, C_RK, C_RV, C_RG = 0, 512, 1024, 2048
C_MQ = 3072
C_GR, C_GA = 4608, 5632
IN_WIDTH = 6656

LANES = 128
VMEM_LIMIT = 56 * 1024 * 1024
HIGHEST = lax.Precision.HIGHEST

NT_DIMS = (((1,), (1,)), ((), ()))
TN_DIMS = (((0,), (0,)), ((), ()))


def _params(n_axes):
    return pltpu.CompilerParams(
        dimension_semantics=("arbitrary",) * n_axes, vmem_limit_bytes=VMEM_LIMIT)


def _dot(a, b, precision=None):
    return jnp.dot(a, b, precision=precision, preferred_element_type=F32)


def _dot_nt(a, b):
    return lax.dot_general(a, b, NT_DIMS, preferred_element_type=F32)


def _dot_tn(a, b):
    return lax.dot_general(a, b, TN_DIMS, preferred_element_type=F32)


def _resident(shape):
    nd = len(shape)
    return pl.BlockSpec(shape, lambda *_: (0,) * nd, pipeline_mode=pl.Buffered(1))


def _layer_resident(stacked, layer):
    nd = stacked.ndim - 1
    return pl.BlockSpec((None,) + stacked.shape[1:], lambda *_: (layer,) + (0,) * nd,
                        pipeline_mode=pl.Buffered(1))


def _rms(x, g):
    return x * lax.rsqrt(jnp.mean(x * x, axis=-1, keepdims=True) + NORM_EPS) * g


def _in_proj_kernel(x_ref, g_ref, w_ref, cr_ref, sr_ref, cmt_ref, smt_ref, *refs, first_layer):
    if first_layer:
        rq_ref, rk_ref, rv_ref, rg_ref, gr_ref, ga_ref, mqt_ref, k_all, v_all, h_scr = refs
        mkt_ref, mvt_ref = k_all.at[0], v_all.at[0]
        for later in range(1, k_all.shape[0]):
            k_all[later] = jnp.zeros(k_all.shape[1:], k_all.dtype)
            v_all[later] = jnp.zeros(v_all.shape[1:], v_all.dtype)
    else:
        _, _, rq_ref, rk_ref, rv_ref, rg_ref, gr_ref, ga_ref, mqt_ref, mkt_ref, mvt_ref, h_scr = refs
    h_scr[...] = _rms(x_ref[...], g_ref[...]).astype(BF16)

    def seg(c0, width=512):
        return _dot(h_scr[...], w_ref[:, c0:c0 + width])

    tm = h_scr.shape[0]
    even_lane = (lax.broadcasted_iota(jnp.int32, (tm, LANES), 1) % 2) == 0
    cr, sr = cr_ref[...], sr_ref[...]

    for out_ref, c0, scale in ((rq_ref, C_RQ, None), (rk_ref, C_RK, R_DK ** -0.5)):
        a = seg(c0)
        for hh in range(R_HEADS):
            sl = slice(hh * LANES, (hh + 1) * LANES)
            ah = a[:, sl]
            partner = jnp.where(even_lane, pltpu.roll(ah, LANES - 1, 1), pltpu.roll(ah, 1, 1))
            r = ah * cr + partner * sr
            if scale is not None:
                r = r * scale
            out_ref[:, sl] = r.astype(out_ref.dtype)
    for j in range(2):
        cols = slice(j * 512, (j + 1) * 512)
        rv_ref[:, cols] = seg(C_RV + j * 512).astype(rv_ref.dtype)
        rg_ref[:, cols] = seg(C_RG + j * 512).astype(rg_ref.dtype)
        gr_ref[:, cols] = seg(C_GR + j * 512).astype(gr_ref.dtype)
        ga_ref[:, cols] = seg(C_GA + j * 512).astype(ga_ref.dtype)

    cmt, smt = cmt_ref[...], smt_ref[...]
    half = M_DH // 2
    for idx, out_ref in enumerate((mqt_ref, mkt_ref, mvt_ref)):
        at = seg(C_MQ + idx * M_W).T
        if idx == 2:
            out_ref[...] = at
            continue
        for hd in range(M_HEADS):
            x1 = at[hd * M_DH:hd * M_DH + half]
            x2 = at[hd * M_DH + half:(hd + 1) * M_DH]
            out_ref[hd * M_DH:hd * M_DH + half, :] = x1 * cmt - x2 * smt
            out_ref[hd * M_DH + half:(hd + 1) * M_DH, :] = x2 * cmt + x1 * smt


def _in_proj(x, g, w, tabs, n_seq, tm, r_dtype, layer, kv_all):
    depth = w.shape[0]
    T = x.shape[0]
    S = T // n_seq
    per_seq = S // tm
    cr, sr, cmt, smt = tabs
    row = lambda i: (i, 0)
    tab_spec = pl.BlockSpec((tm, LANES), lambda i: (i % per_seq, 0))
    tabt_spec = pl.BlockSpec((M_DH // 2, tm), lambda i: (0, i % per_seq))
    widths = (R_QK, R_QK, R_V, R_V, D_MODEL, D_MODEL)
    dtypes = (r_dtype, r_dtype, r_dtype, F32, BF16, BF16)
    featmajor = pl.BlockSpec((None, M_W, tm), lambda i: (i // per_seq, 0, i % per_seq))
    in_specs = [pl.BlockSpec((tm, D_MODEL), row), _layer_resident(g, layer),
                _layer_resident(w, layer), tab_spec, tab_spec, tabt_spec, tabt_spec]
    n_plain = len(widths) + 1
    if layer == 0:
        kv_spec = pl.BlockSpec((depth, None, M_W, tm), lambda i: (0, i // per_seq, 0, i % per_seq))
        extra, aliases = (), {}
    else:
        kv_spec = pl.BlockSpec((None, None, M_W, tm),
                               lambda i: (layer, i // per_seq, 0, i % per_seq))
        extra = tuple(kv_all)
        aliases = {len(in_specs): n_plain, len(in_specs) + 1: n_plain + 1}
        in_specs = in_specs + [pl.BlockSpec(memory_space=pl.ANY)] * 2
    return pl.pallas_call(
        functools.partial(_in_proj_kernel, first_layer=layer == 0),
        grid=(T // tm,),
        in_specs=in_specs,
        out_specs=[pl.BlockSpec((tm, wd), row) for wd in widths] + [featmajor, kv_spec, kv_spec],
        out_shape=[jax.ShapeDtypeStruct((T, wd), dt) for wd, dt in zip(widths, dtypes)]
        + [jax.ShapeDtypeStruct((n_seq, M_W, S), F32)]
        + [jax.ShapeDtypeStruct((depth, n_seq, M_W, S), F32)] * 2,
        scratch_shapes=[pltpu.VMEM((tm, D_MODEL), BF16)],
        input_output_aliases=aliases,
        compiler_params=_params(1),
        name="in_proj",
    )(x, g, w, cr, sr, cmt, smt, *extra)


def _head_rms_gate(o, gate):
    on = o * lax.rsqrt(jnp.mean(o * o, axis=-1, keepdims=True) + NORM_EPS)
    return on * (gate * jax.nn.sigmoid(gate))


def _retention_head(q, k, v, s_prev, dec, qd, kd, gl):
    qf, kf = q.astype(F32), k.astype(F32)
    vb = v.astype(BF16)
    inner = _dot_nt(q.astype(BF16), k.astype(BF16)) * dec
    o = _dot(inner.astype(BF16), vb) + _dot((qf * qd).astype(BF16), s_prev.astype(BF16))
    s_new = gl * s_prev + _dot_tn((kf * kd).astype(BF16), vb)
    return o, s_new


RET_CHUNKS_PER_STEP = 4


def _ret_prompt_kernel(q_ref, k_ref, v_ref, g_ref, dec_ref, qd_ref, kd_ref, gl_ref,
                       o_ref, st_ref, s_scr):
    c = pl.program_id(1)

    @pl.when(c == 0)
    def _():
        s_scr[...] = jnp.zeros_like(s_scr)

    for hh in range(R_HEADS):
        ks = slice(hh * R_DK, (hh + 1) * R_DK)
        vs = slice(hh * R_DV, (hh + 1) * R_DV)
        s = s_scr[hh]
        for j in range(RET_CHUNKS_PER_STEP):
            rows = slice(j * R_CHUNK, (j + 1) * R_CHUNK)
            o, s = _retention_head(q_ref[rows, ks], k_ref[rows, ks], v_ref[rows, vs], s,
                                   dec_ref[hh], qd_ref[hh], kd_ref[hh], gl_ref[hh])
            o_ref[rows, vs] = _head_rms_gate(o, g_ref[rows, vs]).astype(o_ref.dtype)
        s_scr[hh] = s

    @pl.when(c == pl.num_programs(1) - 1)
    def _():
        st_ref[...] = s_scr[...]


def _retention_prompt(rq, rk, rv, rg, tabs, B, S):
    rows = RET_CHUNKS_PER_STEP * R_CHUNK
    nc = S // rows
    row = lambda b, c: (b * nc + c, 0)
    dec, qd, kd, gl = tabs
    return pl.pallas_call(
        _ret_prompt_kernel,
        grid=(B, nc),
        in_specs=[pl.BlockSpec((rows, R_QK), row), pl.BlockSpec((rows, R_QK), row),
                  pl.BlockSpec((rows, R_V), row), pl.BlockSpec((rows, R_V), row),
                  _resident(dec.shape), _resident(qd.shape), _resident(kd.shape),
                  _resident(gl.shape)],
        out_specs=[pl.BlockSpec((rows, R_V), row),
                   pl.BlockSpec((None, R_HEADS, R_DK, R_DV), lambda b, c: (b, 0, 0, 0))],
        out_shape=[jax.ShapeDtypeStruct((B * S, R_V), BF16),
                   jax.ShapeDtypeStruct((B, R_HEADS, R_DK, R_DV), F32)],
        scratch_shapes=[pltpu.VMEM((R_HEADS, R_DK, R_DV), F32)],
        compiler_params=_params(2),
        name="retention_prompt",
    )(rq, rk, rv, rg, dec, qd, kd, gl)


def _ret_sample_kernel(q_ref, k_ref, v_ref, g_ref, s_ref, dec_ref, qd_ref, kd_ref, gl_ref,
                       *refs, first_layer):
    if first_layer:
        o_ref, st_all = refs
        st_ref = st_all.at[0]
        for later in range(1, st_all.shape[0]):
            st_all[later] = jnp.zeros(st_all.shape[1:], st_all.dtype)
    else:
        _, o_ref, st_ref = refs
    for hh in range(R_HEADS):
        ks = slice(hh * R_DK, (hh + 1) * R_DK)
        vs = slice(hh * R_DV, (hh + 1) * R_DV)
        o, s_new = _retention_head(q_ref[:, ks], k_ref[:, ks], v_ref[:, vs], s_ref[hh],
                                   dec_ref[hh], qd_ref[hh], kd_ref[hh], gl_ref[hh])
        st_ref[hh] = s_new
        o_ref[:, vs] = _head_rms_gate(o, g_ref[:, vs]).astype(o_ref.dtype)


def _retention_sample(rq, rk, rv, rg, state_all, layer, tabs, new_states):
    DB, LP = rq.shape[:2]
    depth = state_all.shape[0]
    dec, qd, kd, gl = tabs
    tok = lambda w: pl.BlockSpec((None, LP, w), lambda b: (b, 0, 0))
    st_in = pl.BlockSpec((None, None, R_HEADS, R_DK, R_DV), lambda b: (layer, b, 0, 0, 0))
    in_specs = [tok(R_QK), tok(R_QK), tok(R_V), tok(R_V), st_in,
                _resident(dec.shape), _resident(qd.shape), _resident(kd.shape),
                _resident(gl.shape)]
    if layer == 0:
        st_out = pl.BlockSpec((depth, None, R_HEADS, R_DK, R_DV), lambda b: (0, b, 0, 0, 0))
        extra, aliases = (), {}
    else:
        st_out = st_in
        extra, aliases = (new_states,), {len(in_specs): 1}
        in_specs = in_specs + [pl.BlockSpec(memory_space=pl.ANY)]
    return pl.pallas_call(
        functools.partial(_ret_sample_kernel, first_layer=layer == 0),
        grid=(DB,),
        in_specs=in_specs,
        out_specs=[tok(R_V), st_out],
        out_shape=[jax.ShapeDtypeStruct((DB, LP, R_V), F32),
                   jax.ShapeDtypeStruct(state_all.shape, F32)],
        input_output_aliases=aliases,
        compiler_params=_params(1),
        name="retention_sample",
    )(rq, rk, rv, rg, state_all, dec, qd, kd, gl, *extra)


def _retention_tables(L, l_real):
    log_gamma = jnp.log1p(-jnp.exp2(-5.0 - jnp.arange(R_HEADS, dtype=F32)))
    n = jnp.arange(L, dtype=F32)
    diff = n[:, None] - n[None, :]
    dec = jnp.where(diff >= 0, jnp.exp(log_gamma[:, None, None] * jnp.maximum(diff, 0.0)), 0.0)
    q_decay = jnp.exp(log_gamma[:, None] * (n[None, :] + 1.0))
    k_decay = jnp.exp(log_gamma[:, None] * (l_real - 1.0 - n[None, :]))
    qd = jnp.broadcast_to(q_decay[:, :, None], (R_HEADS, L, R_DK))
    kd = jnp.broadcast_to(k_decay[:, :, None], (R_HEADS, L, R_DK))
    gl = jnp.broadcast_to(jnp.exp(log_gamma * l_real)[:, None, None], (R_HEADS, 1, R_DV))
    return dec, qd, kd, gl


def _rank_select(scores, n_idx, n_cand, axis):
    nb = scores.shape[axis]
    cnt = jnp.zeros(scores.shape, F32)
    for m in range(nb):
        sm = scores[m:m + 1, :] if axis == 0 else scores[:, m:m + 1]
        beats = (sm > scores) | ((sm == scores) & (m < n_idx))
        cnt = cnt + jnp.where(beats, jnp.where(m < n_cand, 1.0, 0.0), 0.0)
    return jnp.where((n_idx < n_cand) & (cnt < float(M_TOPK)), 1.0, 0.0)


BF16_ROWS = 16
V_ROWS = M_DH + BF16_ROWS
LOG2E = 1.4426950408889634
CHUNK_BLOCKS = 4


def _moba_prompt_kernel(qt_ref, kt_ref, vt_ref, o_ref, kaug_scr, vaug_scr, means_scr,
                        s_scr, m_scr, acc_scr):
    i = pl.program_id(2)
    nb = means_scr.shape[0]
    tq = qt_ref.shape[1]
    ck = CHUNK_BLOCKS * M_BLOCK
    own_rows = (slice(0, M_DH), slice(M_DH, LANES))
    onehot_lane0 = (M_DH, 0)

    @pl.when(i == 0)
    def _():
        lane = lax.broadcasted_iota(jnp.int32, (M_BLOCK, LANES), 1)
        ones_tile = jnp.ones((BF16_ROWS, M_BLOCK), BF16)
        for n in range(nb):
            cols = slice(n * M_BLOCK, (n + 1) * M_BLOCK)
            kblk = kt_ref[:, cols].T
            means_scr[n:n + 1, :] = jnp.sum(kblk, axis=0, keepdims=True) * (1.0 / M_BLOCK)
            for hd in range(2):
                mine = (lane < M_DH) if hd == 0 else (lane >= M_DH)
                other = jnp.where(lane == onehot_lane0[hd] + n, 1.0, 0.0)
                kaug_scr[hd, cols, :] = jnp.where(mine, kblk, other).astype(BF16)
                vaug_scr[hd, 0:M_DH, cols] = vt_ref[own_rows[hd], cols].astype(BF16)
                vaug_scr[hd, M_DH:V_ROWS, cols] = ones_tile

    qt = qt_ref[...]
    row = lax.broadcasted_iota(jnp.int32, qt.shape, 0)
    n_idx = lax.broadcasted_iota(jnp.int32, (nb, tq), 0)
    means = means_scr[...]
    pad = jnp.zeros((M_DH - nb, tq), BF16)
    rhs = []
    for hd in range(2):
        mine = (row < M_DH) if hd == 0 else (row >= M_DH)
        qh = jnp.where(mine, qt, 0.0)
        sel = _rank_select(_dot(means, qh, precision=HIGHEST), n_idx, i, axis=0)
        bias = jnp.where((sel > 0.5) | (n_idx == i), 0.0, NEG).astype(BF16)
        q_own = (qt[own_rows[hd]] * (M_DH ** -0.5 * LOG2E)).astype(BF16)
        parts = [q_own, bias, pad]
        rhs.append(jnp.concatenate(parts if hd == 0 else parts[1:] + parts[:1], axis=0))

    def chunk_scores(c, hd, with_causal):
        s = _dot(kaug_scr[hd, c * ck:(c + 1) * ck, :], rhs[hd])
        if with_causal:
            key = lax.broadcasted_iota(jnp.int32, (ck, tq), 0)
            qry = lax.broadcasted_iota(jnp.int32, (1, tq), 1) + (i - c * CHUNK_BLOCKS) * M_BLOCK
            s = jnp.where(key <= qry, s, NEG)
        s_scr[hd, c] = s
        m = jnp.max(s, axis=0, keepdims=True)
        m_scr[hd] = m if c == 0 else jnp.maximum(m_scr[hd], m)

    for c in range(nb // CHUNK_BLOCKS):
        @pl.when((c + 1) * CHUNK_BLOCKS <= i)
        def _():
            for hd in range(2):
                chunk_scores(c, hd, False)

        @pl.when((c * CHUNK_BLOCKS <= i) & (i < (c + 1) * CHUNK_BLOCKS))
        def _():
            for hd in range(2):
                chunk_scores(c, hd, True)

    for c in range(nb // CHUNK_BLOCKS):
        @pl.when(c * CHUNK_BLOCKS <= i)
        def _():
            for hd in range(2):
                p = jnp.exp2(s_scr[hd, c] - m_scr[hd]).astype(BF16)
                pv = _dot(vaug_scr[hd, :, c * ck:(c + 1) * ck], p)
                acc_scr[hd] = pv if c == 0 else acc_scr[hd] + pv

    o_t = jnp.concatenate([acc_scr[0, 0:M_DH, :] / acc_scr[0, M_DH:M_DH + 1, :],
                           acc_scr[1, 0:M_DH, :] / acc_scr[1, M_DH:M_DH + 1, :]], axis=0)
    o_ref[...] = o_t.T.astype(o_ref.dtype)


def _moba_prompt(mqt, k_all, v_all, layer):
    B, _, S = mqt.shape
    nb = S // M_BLOCK
    tq = M_BLOCK
    npair = M_W // LANES
    assert nb % CHUNK_BLOCKS == 0 and nb <= M_DH and nb % BF16_ROWS == 0
    return pl.pallas_call(
        _moba_prompt_kernel,
        grid=(B, npair, nb),
        in_specs=[pl.BlockSpec((None, LANES, tq), lambda b, p, i: (b, p, i)),
                  pl.BlockSpec((None, None, LANES, S), lambda b, p, i: (layer, b, p, 0)),
                  pl.BlockSpec((None, None, LANES, S), lambda b, p, i: (layer, b, p, 0))],
        out_specs=pl.BlockSpec((tq, LANES), lambda b, p, i: (b * nb + i, p)),
        out_shape=jax.ShapeDtypeStruct((B * S, M_W), BF16),
        scratch_shapes=[pltpu.VMEM((2, S, LANES), BF16),
                        pltpu.VMEM((2, V_ROWS, S), BF16),
                        pltpu.VMEM((nb, LANES), F32),
                        pltpu.VMEM((2, nb // CHUNK_BLOCKS, CHUNK_BLOCKS * M_BLOCK, tq), F32),
                        pltpu.VMEM((2, 1, tq), F32),
                        pltpu.VMEM((2, V_ROWS, tq), F32)],
        compiler_params=_params(3),
        name="moba_prompt",
    )(mqt, k_all, v_all)


PAGES_PER_STEP = 64


def _own_head(shape):
    row = lax.broadcasted_iota(jnp.int32, shape, 0)
    lane = lax.broadcasted_iota(jnp.int32, shape, 1)
    return (lane // M_DH) == (row % M_HEADS)


def _moba_keys_kernel(pt_ref, q_ref, *refs):
    k_refs = refs[:PAGES_PER_STEP]
    s_ref, bsum_ref = refs[PAGES_PER_STEP:]
    step = pl.program_id(1)

    @pl.when(step == 0)
    def _():
        bsum_ref[...] = jnp.zeros_like(bsum_ref)

    q = q_ref[...]
    qb = (jnp.where(_own_head(q.shape), q, 0.0) * (M_DH ** -0.5)).astype(BF16)
    ppb = M_BLOCK // PAGE_SIZE
    blk_lane = lax.broadcasted_iota(jnp.int32, bsum_ref.shape, 1)
    bsum = bsum_ref[...]
    for j in range(PAGES_PER_STEP):
        kpt = k_refs[j][...].reshape(M_W, PAGE_SIZE)
        s_ref[:, j * PAGE_SIZE:(j + 1) * PAGE_SIZE] = _dot(qb, kpt.astype(BF16))
        blk_acc = kpt if j % ppb == 0 else blk_acc + kpt
        if j % ppb == ppb - 1:
            rowsum = jnp.sum(blk_acc, axis=1, keepdims=True)
            n = step * (PAGES_PER_STEP // ppb) + j // ppb
            bsum = bsum + jnp.where(blk_lane == n, rowsum, 0.0)
    bsum_ref[...] = bsum


def _page_specs(layer):
    def spec(j):
        return pl.BlockSpec(
            (None, None, M_HEADS, M_DH, PAGE_SIZE),
            lambda b, s, pt: (layer, pt[b, s * PAGES_PER_STEP + j], 0, 0, 0))
    return [spec(j) for j in range(PAGES_PER_STEP)]


def _moba_keys(page_table, q_rep, cache_kt, layer):
    DB, n_pages = page_table.shape
    R = q_rep.shape[1]
    steps = n_pages // PAGES_PER_STEP
    nb = n_pages * PAGE_SIZE // M_BLOCK
    grid_spec = pltpu.PrefetchScalarGridSpec(
        num_scalar_prefetch=1,
        grid=(DB, steps),
        in_specs=[pl.BlockSpec((None, R, M_W), lambda b, s, pt: (b, 0, 0))] + _page_specs(layer),
        out_specs=[pl.BlockSpec((None, R, PAGES_PER_STEP * PAGE_SIZE), lambda b, s, pt: (b, 0, s)),
                   pl.BlockSpec((None, M_W, nb), lambda b, s, pt: (b, 0, 0))],
    )
    return pl.pallas_call(
        _moba_keys_kernel,
        grid_spec=grid_spec,
        out_shape=[jax.ShapeDtypeStruct((DB, R, n_pages * PAGE_SIZE), F32),
                   jax.ShapeDtypeStruct((DB, M_W, nb), F32)],
        compiler_params=_params(2),
        name="moba_sample_keys",
    )(page_table, q_rep, *([cache_kt] * PAGES_PER_STEP))


def _sample_probs(s_ref, bsum_ref, q_ref, kn_ref, vn_ref, p_ref):
    q = q_ref[...]
    R = q.shape[0]
    qh = jnp.where(_own_head(q.shape), q, 0.0)
    means_t = bsum_ref[...] * (1.0 / M_BLOCK)
    nb = means_t.shape[1]
    sc = _dot(qh, means_t, precision=HIGHEST)
    n_idx = lax.broadcasted_iota(jnp.int32, (R, nb), 1)
    sel = _rank_select(sc, n_idx, nb, axis=1)
    s = s_ref[...]
    past = s.shape[1]
    blk_of_key = lax.broadcasted_iota(jnp.int32, (nb, past), 1) // M_BLOCK
    expand = jnp.where(blk_of_key == lax.broadcasted_iota(jnp.int32, (nb, past), 0), 1.0, 0.0)
    keep = _dot(sel.astype(BF16), expand.astype(BF16)) > 0.5
    s = jnp.where(keep, s, NEG)
    qb = (qh * (M_DH ** -0.5)).astype(BF16)
    s_own = _dot_nt(qb, kn_ref[...].astype(BF16))
    tok = lax.broadcasted_iota(jnp.int32, s_own.shape, 0) // M_HEADS
    s_own = jnp.where(lax.broadcasted_iota(jnp.int32, s_own.shape, 1) <= tok, s_own, NEG)
    m = jnp.maximum(jnp.max(s, axis=1, keepdims=True), jnp.max(s_own, axis=1, keepdims=True))
    p = jnp.exp(s - m)
    p_own = jnp.exp(s_own - m)
    inv = 1.0 / (jnp.sum(p, axis=1, keepdims=True) + jnp.sum(p_own, axis=1, keepdims=True))
    p_ref[...] = (p * inv).astype(p_ref.dtype)
    return _dot((p_own * inv).astype(BF16), vn_ref[...].astype(BF16))


def _moba_values_kernel(pt_ref, s_ref, bsum_ref, q_ref, kn_ref, vn_ref, *refs):
    v_refs = refs[:PAGES_PER_STEP]
    o_ref, p_scr = refs[PAGES_PER_STEP:]
    acc = _sample_probs(s_ref, bsum_ref, q_ref, kn_ref, vn_ref, p_scr)
    for j in range(PAGES_PER_STEP):
        vpt = v_refs[j][...].reshape(M_W, PAGE_SIZE).astype(BF16)
        acc = acc + _dot_nt(p_scr[:, j * PAGE_SIZE:(j + 1) * PAGE_SIZE], vpt)
    R = acc.shape[0]
    picked = jnp.where(_own_head(acc.shape), acc, 0.0)
    o_ref[...] = jnp.sum(picked.reshape(R // M_HEADS, M_HEADS, M_W), axis=1).astype(o_ref.dtype)


def _moba_values(page_table, scores, bsum_t, q_rep, k_new, v_new, cache_vt, layer):
    DB, n_pages = page_table.shape
    _, R, past = scores.shape
    nb = bsum_t.shape[2]
    LP = k_new.shape[1]
    assert n_pages == PAGES_PER_STEP
    per_b = lambda *shape: pl.BlockSpec((None,) + shape, lambda b, s, pt: (b, 0, 0))
    grid_spec = pltpu.PrefetchScalarGridSpec(
        num_scalar_prefetch=1,
        grid=(DB, 1),
        in_specs=[per_b(R, past), per_b(M_W, nb), per_b(R, M_W), per_b(LP, M_W), per_b(LP, M_W)]
        + _page_specs(layer),
        out_specs=per_b(R // M_HEADS, M_W),
        scratch_shapes=[pltpu.VMEM((R, past), BF16)],
    )
    return pl.pallas_call(
        _moba_values_kernel,
        grid_spec=grid_spec,
        out_shape=jax.ShapeDtypeStruct((DB, R // M_HEADS, M_W), F32),
        compiler_params=_params(2),
        name="moba_sample_values",
    )(page_table, scores, bsum_t, q_rep, k_new, v_new, *([cache_vt] * PAGES_PER_STEP))


def _merge_kernel(x_ref, ro_ref, mo_ref, gr_ref, ga_ref, wr_ref, wa_ref, wo_ref, y_ref):
    merged = (jax.nn.sigmoid(gr_ref[...].astype(F32)) * _dot(ro_ref[...].astype(BF16), wr_ref[...])
              + jax.nn.sigmoid(ga_ref[...].astype(F32)) * _dot(mo_ref[...].astype(BF16), wa_ref[...]))
    y_ref[...] = x_ref[...] + _dot(merged.astype(BF16), wo_ref[...])


def _merge(x, ro, mo, gr, ga, wr, wa, wo, layer, tm):
    T = x.shape[0]
    row = lambda i: (i, 0)
    blk = lambda w: pl.BlockSpec((tm, w), row)
    return pl.pallas_call(
        _merge_kernel,
        grid=(T // tm,),
        in_specs=[blk(D_MODEL), blk(R_V), blk(M_W), blk(D_MODEL), blk(D_MODEL),
                  _layer_resident(wr, layer), _layer_resident(wa, layer),
                  _layer_resident(wo, layer)],
        out_specs=blk(D_MODEL),
        out_shape=jax.ShapeDtypeStruct((T, D_MODEL), F32),
        compiler_params=_params(1),
        name="merge",
    )(x, ro, mo, gr, ga, wr, wa, wo)


FF_CHUNK = 256


def _ffn_kernel(x_ref, g_ref, wgu_ref, wd_ref, gf_ref, y_ref, h_scr, act_scr, *, final_norm):
    x = x_ref[...]
    h_scr[...] = _rms(x, g_ref[...]).astype(BF16)
    for c0 in range(0, D_FF, FF_CHUNK):
        gate = _dot(h_scr[...], wgu_ref[:, c0:c0 + FF_CHUNK])
        up = _dot(h_scr[...], wgu_ref[:, D_FF + c0:D_FF + c0 + FF_CHUNK])
        act_scr[:, c0:c0 + FF_CHUNK] = (gate * jax.nn.sigmoid(gate) * up).astype(BF16)
    y = x + _dot(act_scr[...], wd_ref[...])
    if final_norm:
        y = _rms(y, gf_ref[...])
    y_ref[...] = y


def _ffn(x, g, wgu, wd, g_final, layer, tm, final_norm):
    T = x.shape[0]
    row = lambda i: (i, 0)
    return pl.pallas_call(
        functools.partial(_ffn_kernel, final_norm=final_norm),
        grid=(T // tm,),
        in_specs=[pl.BlockSpec((tm, D_MODEL), row), _layer_resident(g, layer),
                  _layer_resident(wgu, layer), _layer_resident(wd, layer),
                  _resident((1, D_MODEL))],
        out_specs=pl.BlockSpec((tm, D_MODEL), row),
        out_shape=jax.ShapeDtypeStruct((T, D_MODEL), F32),
        scratch_shapes=[pltpu.VMEM((tm, D_MODEL), BF16), pltpu.VMEM((tm, D_FF), BF16)],
        compiler_params=_params(1),
        name="ffn",
    )(x, g, wgu, wd, g_final)


def _rotation_tables(pos):
    posf = pos.astype(F32)
    angle = 1.0 / (10000.0 ** jnp.linspace(0.0, 1.0, R_DK // 2, dtype=F32))
    ang = jnp.repeat(posf[:, None] * angle[None, :], 2, axis=-1)
    sign = jnp.where(jnp.arange(R_DK) % 2 == 0, -1.0, 1.0).astype(F32)
    half = M_DH // 2
    inv = ROPE_THETA ** (-jnp.arange(half, dtype=F32) / half)
    ang_m = (posf[:, None] * inv[None, :]).T
    return jnp.cos(ang), jnp.sin(ang) * sign[None, :], jnp.cos(ang_m), jnp.sin(ang_m)


def _layer_tail(x, ro, mo, gr, ga, wts, l, tm, final_norm):
    x = _merge(x, ro, mo, gr, ga, wts["w_br_ret"], wts["w_br_att"], wts["w_out"], l, tm)
    return _ffn(x, wts["g_ffn"], wts["w_gu"], wts["w_down"], wts["g_final"], l, tm, final_norm)


def _feature_major_rows(t, n_seq):
    depth, _, _, S = t.shape
    return t.reshape(depth, n_seq, M_HEADS, M_DH, S).transpose(0, 1, 4, 2, 3)


def _prompt_group(x_prompt, wts):
    B, S, _ = x_prompt.shape
    depth = wts["w_in"].shape[0]
    tm = 512
    x = x_prompt.reshape(B * S, D_MODEL)
    rot = _rotation_tables(jnp.arange(S))
    ret_tabs = _retention_tables(R_CHUNK, float(R_CHUNK))
    kv_all, states = None, []
    for l in range(depth):
        rq, rk, rv, rg, gr, ga, mqt, *kv_all = _in_proj(
            x, wts["g_mix"], wts["w_in"], rot, B, tm, BF16, l, kv_all)
        ro, st = _retention_prompt(rq, rk, rv, rg, ret_tabs, B, S)
        mo = _moba_prompt(mqt, kv_all[0], kv_all[1], l)
        x = _layer_tail(x, ro, mo, gr, ga, wts, l, tm, l == depth - 1)
        states.append(st)
    return (x.reshape(B, S, D_MODEL), _feature_major_rows(kv_all[0], B),
            _feature_major_rows(kv_all[1], B), jnp.stack(states))


def _sample_group(x_sample, cache_k, cache_v, state_ret, page_table, wts):
    DB, L, _ = x_sample.shape
    depth = wts["w_in"].shape[0]
    T = DB * L
    LP = 8
    past_len = page_table.shape[1] * PAGE_SIZE
    x = x_sample.reshape(T, D_MODEL)
    rot = _rotation_tables(past_len + (jnp.arange(T) % L))
    ret_tabs = _retention_tables(LP, float(L))
    ckt = cache_k.transpose(0, 1, 3, 4, 2)
    cvt = cache_v.transpose(0, 1, 3, 4, 2)

    def per_seq(t):
        return jnp.pad(t.reshape(DB, L, -1), ((0, 0), (0, LP - L), (0, 0)))

    kv_all, new_states = None, None
    for l in range(depth):
        rq, rk, rv, rg, gr, ga, mqt, *kv_all = _in_proj(
            x, wts["g_mix"], wts["w_in"], rot, 1, T, F32, l, kv_all)
        mq, mk, mv = mqt[0].T, kv_all[0][l, 0].T, kv_all[1][l, 0].T
        ro, new_states = _retention_sample(per_seq(rq), per_seq(rk), per_seq(rv), per_seq(rg),
                                           state_ret, l, ret_tabs, new_states)
        q_rep = jnp.repeat(mq.reshape(DB, L, M_W), M_HEADS, axis=1)
        scores, bsum_t = _moba_keys(page_table, q_rep, ckt, l)
        mo = _moba_values(page_table, scores, bsum_t, q_rep, per_seq(mk), per_seq(mv), cvt, l)
        x = _layer_tail(x, ro[:, :L].reshape(T, R_V), mo.reshape(T, M_W), gr, ga, wts, l, T,
                        l == depth - 1)

    def new_rows(t):
        return t[:, 0].transpose(0, 2, 1).reshape(depth, DB, L, M_HEADS, M_DH)

    return x.reshape(DB, L, D_MODEL), new_rows(kv_all[0]), new_rows(kv_all[1]), new_states


def kernel(x_prompt, x_sample, cache_k, cache_v, state_ret, page_table, g_mix, w_in, w_br_ret,
           w_br_att, w_out, g_ffn, w_gu, w_down, g_final):
    depth = w_in.shape[0]
    wts = {
        "g_mix": g_mix.reshape(depth, 1, D_MODEL), "g_ffn": g_ffn.reshape(depth, 1, D_MODEL),
        "g_final": g_final.reshape(1, D_MODEL),
        "w_in": w_in.astype(BF16),
        "w_br_ret": w_br_ret.astype(BF16), "w_br_att": w_br_att.astype(BF16),
        "w_out": w_out.astype(BF16), "w_gu": w_gu.astype(BF16), "w_down": w_down.astype(BF16),
    }
    y_p, k_p, v_p, st_p = _prompt_group(x_prompt, wts)
    y_s, k_s, v_s, st_s = _sample_group(x_sample, cache_k, cache_v, state_ret, page_table, wts)
    return (y_p, y_s, k_p, v_p, st_p.astype(state_ret.dtype), k_s, v_s, st_s.astype(state_ret.dtype))
```

```python
import functools

import jax
import jax.numpy as jnp
from jax import lax
from jax.experimental import pallas as pl
from jax.experimental.pallas import tpu as pltpu

F32 = jnp.float32
BF16 = jnp.bfloat16

D_MODEL = 1024
R_HEADS = 4
R_DK = 128
R_DV = 256
R_CHUNK = 128
R_QK = R_HEADS * R_DK
R_V = R_HEADS * R_DV
M_HEADS = 8
M_DH = 64
M_W = M_HEADS * M_DH
M_BLOCK = 256
M_TOPK = 3
PAGE_SIZE = 128
ROPE_THETA = 10000.0
D_FF = 2816
NORM_EPS = 1e-6
NEG = -1e30

C_RQ, C_RK, C_RV, C_RG = 0, 512, 1024, 2048
C_MQ = 3072
C_GR, C_GA = 4608, 5632
IN_WIDTH = 6656

LANES = 128
VMEM_LIMIT = 56 * 1024 * 1024
HIGHEST = lax.Precision.HIGHEST

NT_DIMS = (((1,), (1,)), ((), ()))
TN_DIMS = (((0,), (0,)), ((), ()))


def _params(n_axes):
    return pltpu.CompilerParams(
        dimension_semantics=("arbitrary",) * n_axes, vmem_limit_bytes=VMEM_LIMIT)


def _dot(a, b, precision=None):
    return jnp.dot(a, b, precision=precision, preferred_element_type=F32)


def _dot_nt(a, b):
    return lax.dot_general(a, b, NT_DIMS, preferred_element_type=F32)


def _dot_tn(a, b):
    return lax.dot_general(a, b, TN_DIMS, preferred_element_type=F32)


def _resident(shape):
    nd = len(shape)
    return pl.BlockSpec(shape, lambda *_: (0,) * nd, pipeline_mode=pl.Buffered(1))


def _layer_resident(stacked, layer):
    nd = stacked.ndim - 1
    return pl.BlockSpec((None,) + stacked.shape[1:], lambda *_: (layer,) + (0,) * nd,
                        pipeline_mode=pl.Buffered(1))


def _rms(x, g):
    return x * lax.rsqrt(jnp.mean(x * x, axis=-1, keepdims=True) + NORM_EPS) * g


def _in_proj_kernel(x_ref, g_ref, w_ref, cr_ref, sr_ref, cmt_ref, smt_ref, *refs, first_layer):
    if first_layer:
        rq_ref, rk_ref, rv_ref, rg_ref, gr_ref, ga_ref, mqt_ref, k_all, v_all, h_scr = refs
        mkt_ref, mvt_ref = k_all.at[0], v_all.at[0]
        for later in range(1, k_all.shape[0]):
            k_all[later] = jnp.zeros(k_all.shape[1:], k_all.dtype)
            v_all[later] = jnp.zeros(v_all.shape[1:], v_all.dtype)
    else:
        _, _, rq_ref, rk_ref, rv_ref, rg_ref, gr_ref, ga_ref, mqt_ref, mkt_ref, mvt_ref, h_scr = refs
    h_scr[...] = _rms(x_ref[...], g_ref[...]).astype(BF16)

    def seg(c0, width=512):
        return _dot(h_scr[...], w_ref[:, c0:c0 + width])

    tm = h_scr.shape[0]
    even_lane = (lax.broadcasted_iota(jnp.int32, (tm, LANES), 1) % 2) == 0
    cr, sr = cr_ref[...], sr_ref[...]

    for out_ref, c0, scale in ((rq_ref, C_RQ, None), (rk_ref, C_RK, R_DK ** -0.5)):
        a = seg(c0)
        for hh in range(R_HEADS):
            sl = slice(hh * LANES, (hh + 1) * LANES)
            ah = a[:, sl]
            partner = jnp.where(even_lane, pltpu.roll(ah, LANES - 1, 1), pltpu.roll(ah, 1, 1))
            r = ah * cr + partner * sr
            if scale is not None:
                r = r * scale
            out_ref[:, sl] = r.astype(out_ref.dtype)
    for j in range(2):
        cols = slice(j * 512, (j + 1) * 512)
        rv_ref[:, cols] = seg(C_RV + j * 512).astype(rv_ref.dtype)
        rg_ref[:, cols] = seg(C_RG + j * 512).astype(rg_ref.dtype)
        gr_ref[:, cols] = seg(C_GR + j * 512).astype(gr_ref.dtype)
        ga_ref[:, cols] = seg(C_GA + j * 512).astype(ga_ref.dtype)

    cmt, smt = cmt_ref[...], smt_ref[...]
    half = M_DH // 2
    for idx, out_ref in enumerate((mqt_ref, mkt_ref, mvt_ref)):
        at = seg(C_MQ + idx * M_W).T
        if idx == 2:
            out_ref[...] = at
            continue
        for hd in range(M_HEADS):
            x1 = at[hd * M_DH:hd * M_DH + half]
            x2 = at[hd * M_DH + half:(hd + 1) * M_DH]
            out_ref[hd * M_DH:hd * M_DH + half, :] = x1 * cmt - x2 * smt
            out_ref[hd * M_DH + half:(hd + 1) * M_DH, :] = x2 * cmt + x1 * smt


def _in_proj(x, g, w, tabs, n_seq, tm, r_dtype, layer, kv_all):
    depth = w.shape[0]
    T = x.shape[0]
    S = T // n_seq
    per_seq = S // tm
    cr, sr, cmt, smt = tabs
    row = lambda i: (i, 0)
    tab_spec = pl.BlockSpec((tm, LANES), lambda i: (i % per_seq, 0))
    tabt_spec = pl.BlockSpec((M_DH // 2, tm), lambda i: (0, i % per_seq))
    widths = (R_QK, R_QK, R_V, R_V, D_MODEL, D_MODEL)
    dtypes = (r_dtype, r_dtype, r_dtype, F32, BF16, BF16)
    featmajor = pl.BlockSpec((None, M_W, tm), lambda i: (i // per_seq, 0, i % per_seq))
    in_specs = [pl.BlockSpec((tm, D_MODEL), row), _layer_resident(g, layer),
                _layer_resident(w, layer), tab_spec, tab_spec, tabt_spec, tabt_spec]
    n_plain = len(widths) + 1
    if layer == 0:
        kv_spec = pl.BlockSpec((depth, None, M_W, tm), lambda i: (0, i // per_seq, 0, i % per_seq))
        extra, aliases = (), {}
    else:
        kv_spec = pl.BlockSpec((None, None, M_W, tm),
                               lambda i: (layer, i // per_seq, 0, i % per_seq))
        extra = tuple(kv_all)
        aliases = {len(in_specs): n_plain, len(in_specs) + 1: n_plain + 1}
        in_specs = in_specs + [pl.BlockSpec(memory_space=pl.ANY)] * 2
    return pl.pallas_call(
        functools.partial(_in_proj_kernel, first_layer=layer == 0),
        grid=(T // tm,),
        in_specs=in_specs,
        out_specs=[pl.BlockSpec((tm, wd), row) for wd in widths] + [featmajor, kv_spec, kv_spec],
        out_shape=[jax.ShapeDtypeStruct((T, wd), dt) for wd, dt in zip(widths, dtypes)]
        + [jax.ShapeDtypeStruct((n_seq, M_W, S), F32)]
        + [jax.ShapeDtypeStruct((depth, n_seq, M_W, S), F32)] * 2,
        scratch_shapes=[pltpu.VMEM((tm, D_MODEL), BF16)],
        input_output_aliases=aliases,
        compiler_params=_params(1),
        name="in_proj",
    )(x, g, w, cr, sr, cmt, smt, *extra)


def _head_rms_gate(o, gate):
    on = o * lax.rsqrt(jnp.mean(o * o, axis=-1, keepdims=True) + NORM_EPS)
    return on * (gate * jax.nn.sigmoid(gate))


def _retention_head(q, k, v, s_prev, dec, qd, kd, gl):
    qf, kf = q.astype(F32), k.astype(F32)
    vb = v.astype(BF16)
    inner = _dot_nt(q.astype(BF16), k.astype(BF16)) * dec
    o = _dot(inner.astype(BF16), vb) + _dot((qf * qd).astype(BF16), s_prev.astype(BF16))
    s_new = gl * s_prev + _dot_tn((kf * kd).astype(BF16), vb)
    return o, s_new


RET_CHUNKS_PER_STEP = 4


def _ret_prompt_kernel(q_ref, k_ref, v_ref, g_ref, dec_ref, qd_ref, kd_ref, gl_ref,
                       o_ref, st_ref, s_scr):
    c = pl.program_id(1)

    @pl.when(c == 0)
    def _():
        s_scr[...] = jnp.zeros_like(s_scr)

    for hh in range(R_HEADS):
        ks = slice(hh * R_DK, (hh + 1) * R_DK)
        vs = slice(hh * R_DV, (hh + 1) * R_DV)
        s = s_scr[hh]
        for j in range(RET_CHUNKS_PER_STEP):
            rows = slice(j * R_CHUNK, (j + 1) * R_CHUNK)
            o, s = _retention_head(q_ref[rows, ks], k_ref[rows, ks], v_ref[rows, vs], s,
                                   dec_ref[hh], qd_ref[hh], kd_ref[hh], gl_ref[hh])
            o_ref[rows, vs] = _head_rms_gate(o, g_ref[rows, vs]).astype(o_ref.dtype)
        s_scr[hh] = s

    @pl.when(c == pl.num_programs(1) - 1)
    def _():
        st_ref[...] = s_scr[...]


def _retention_prompt(rq, rk, rv, rg, tabs, B, S):
    rows = RET_CHUNKS_PER_STEP * R_CHUNK
    nc = S // rows
    row = lambda b, c: (b * nc + c, 0)
    dec, qd, kd, gl = tabs
    return pl.pallas_call(
        _ret_prompt_kernel,
        grid=(B, nc),
        in_specs=[pl.BlockSpec((rows, R_QK), row), pl.BlockSpec((rows, R_QK), row),
                  pl.BlockSpec((rows, R_V), row), pl.BlockSpec((rows, R_V), row),
                  _resident(dec.shape), _resident(qd.shape), _resident(kd.shape),
                  _resident(gl.shape)],
        out_specs=[pl.BlockSpec((rows, R_V), row),
                   pl.BlockSpec((None, R_HEADS, R_DK, R_DV), lambda b, c: (b, 0, 0, 0))],
        out_shape=[jax.ShapeDtypeStruct((B * S, R_V), BF16),
                   jax.ShapeDtypeStruct((B, R_HEADS, R_DK, R_DV), F32)],
        scratch_shapes=[pltpu.VMEM((R_HEADS, R_DK, R_DV), F32)],
        compiler_params=_params(2),
        name="retention_prompt",
    )(rq, rk, rv, rg, dec, qd, kd, gl)


def _ret_sample_kernel(q_ref, k_ref, v_ref, g_ref, s_ref, dec_ref, qd_ref, kd_ref, gl_ref,
                       *refs, first_layer):
    if first_layer:
        o_ref, st_all = refs
        st_ref = st_all.at[0]
        for later in range(1, st_all.shape[0]):
            st_all[later] = jnp.zeros(st_all.shape[1:], st_all.dtype)
    else:
        _, o_ref, st_ref = refs
    for hh in range(R_HEADS):
        ks = slice(hh * R_DK, (hh + 1) * R_DK)
        vs = slice(hh * R_DV, (hh + 1) * R_DV)
        o, s_new = _retention_head(q_ref[:, ks], k_ref[:, ks], v_ref[:, vs], s_ref[hh],
                                   dec_ref[hh], qd_ref[hh], kd_ref[hh], gl_ref[hh])
        st_ref[hh] = s_new
        o_ref[:, vs] = _head_rms_gate(o, g_ref[:, vs]).astype(o_ref.dtype)


def _retention_sample(rq, rk, rv, rg, state_all, layer, tabs, new_states):
    DB, LP = rq.shape[:2]
    depth = state_all.shape[0]
    dec, qd, kd, gl = tabs
    tok = lambda w: pl.BlockSpec((None, LP, w), lambda b: (b, 0, 0))
    st_in = pl.BlockSpec((None, None, R_HEADS, R_DK, R_DV), lambda b: (layer, b, 0, 0, 0))
    in_specs = [tok(R_QK), tok(R_QK), tok(R_V), tok(R_V), st_in,
                _resident(dec.shape), _resident(qd.shape), _resident(kd.shape),
                _resident(gl.shape)]
    if layer == 0:
        st_out = pl.BlockSpec((depth, None, R_HEADS, R_DK, R_DV), lambda b: (0, b, 0, 0, 0))
        extra, aliases = (), {}
    else:
        st_out = st_in
        extra, aliases = (new_states,), {len(in_specs): 1}
        in_specs = in_specs + [pl.BlockSpec(memory_space=pl.ANY)]
    return pl.pallas_call(
        functools.partial(_ret_sample_kernel, first_layer=layer == 0),
        grid=(DB,),
        in_specs=in_specs,
        out_specs=[tok(R_V), st_out],
        out_shape=[jax.ShapeDtypeStruct((DB, LP, R_V), F32),
                   jax.ShapeDtypeStruct(state_all.shape, F32)],
        input_output_aliases=aliases,
        compiler_params=_params(1),
        name="retention_sample",
    )(rq, rk, rv, rg, state_all, dec, qd, kd, gl, *extra)


def _retention_tables(L, l_real):
    log_gamma = jnp.log1p(-jnp.exp2(-5.0 - jnp.arange(R_HEADS, dtype=F32)))
    n = jnp.arange(L, dtype=F32)
    diff = n[:, None] - n[None, :]
    dec = jnp.where(diff >= 0, jnp.exp(log_gamma[:, None, None] * jnp.maximum(diff, 0.0)), 0.0)
    q_decay = jnp.exp(log_gamma[:, None] * (n[None, :] + 1.0))
    k_decay = jnp.exp(log_gamma[:, None] * (l_real - 1.0 - n[None, :]))
    qd = jnp.broadcast_to(q_decay[:, :, None], (R_HEADS, L, R_DK))
    kd = jnp.broadcast_to(k_decay[:, :, None], (R_HEADS, L, R_DK))
    gl = jnp.broadcast_to(jnp.exp(log_gamma * l_real)[:, None, None], (R_HEADS, 1, R_DV))
    return dec, qd, kd, gl


def _rank_select(scores, n_idx, n_cand, axis):
    nb = scores.shape[axis]
    cnt = jnp.zeros(scores.shape, F32)
    for m in range(nb):
        sm = scores[m:m + 1, :] if axis == 0 else scores[:, m:m + 1]
        beats = (sm > scores) | ((sm == scores) & (m < n_idx))
        cnt = cnt + jnp.where(beats, jnp.where(m < n_cand, 1.0, 0.0), 0.0)
    return jnp.where((n_idx < n_cand) & (cnt < float(M_TOPK)), 1.0, 0.0)


BF16_ROWS = 16
V_ROWS = M_DH + BF16_ROWS
LOG2E = 1.4426950408889634
CHUNK_BLOCKS = 4


def _moba_prompt_kernel(qt_ref, kt_ref, vt_ref, o_ref, kaug_scr, vaug_scr, means_scr,
                        s_scr, m_scr, acc_scr):
    i = pl.program_id(2)
    nb = means_scr.shape[0]
    tq = qt_ref.shape[1]
    ck = CHUNK_BLOCKS * M_BLOCK
    own_rows = (slice(0, M_DH), slice(M_DH, LANES))
    onehot_lane0 = (M_DH, 0)

    @pl.when(i == 0)
    def _():
        lane = lax.broadcasted_iota(jnp.int32, (M_BLOCK, LANES), 1)
        ones_tile = jnp.ones((BF16_ROWS, M_BLOCK), BF16)
        for n in range(nb):
            cols = slice(n * M_BLOCK, (n + 1) * M_BLOCK)
            kblk = kt_ref[:, cols].T
            means_scr[n:n + 1, :] = jnp.sum(kblk, axis=0, keepdims=True) * (1.0 / M_BLOCK)
            for hd in range(2):
                mine = (lane < M_DH) if hd == 0 else (lane >= M_DH)
                other = jnp.where(lane == onehot_lane0[hd] + n, 1.0, 0.0)
                kaug_scr[hd, cols, :] = jnp.where(mine, kblk, other).astype(BF16)
                vaug_scr[hd, 0:M_DH, cols] = vt_ref[own_rows[hd], cols].astype(BF16)
                vaug_scr[hd, M_DH:V_ROWS, cols] = ones_tile

    qt = qt_ref[...]
    row = lax.broadcasted_iota(jnp.int32, qt.shape, 0)
    n_idx = lax.broadcasted_iota(jnp.int32, (nb, tq), 0)
    means = means_scr[...]
    pad = jnp.zeros((M_DH - nb, tq), BF16)
    rhs = []
    for hd in range(2):
        mine = (row < M_DH) if hd == 0 else (row >= M_DH)
        qh = jnp.where(mine, qt, 0.0)
        sel = _rank_select(_dot(means, qh, precision=HIGHEST), n_idx, i, axis=0)
        bias = jnp.where((sel > 0.5) | (n_idx == i), 0.0, NEG).astype(BF16)
        q_own = (qt[own_rows[hd]] * (M_DH ** -0.5 * LOG2E)).astype(BF16)
        parts = [q_own, bias, pad]
        rhs.append(jnp.concatenate(parts if hd == 0 else parts[1:] + parts[:1], axis=0))

    def chunk_scores(c, hd, with_causal):
        s = _dot(kaug_scr[hd, c * ck:(c + 1) * ck, :], rhs[hd])
        if with_causal:
            key = lax.broadcasted_iota(jnp.int32, (ck, tq), 0)
            qry = lax.broadcasted_iota(jnp.int32, (1, tq), 1) + (i - c * CHUNK_BLOCKS) * M_BLOCK
            s = jnp.where(key <= qry, s, NEG)
        s_scr[hd, c] = s
        m = jnp.max(s, axis=0, keepdims=True)
        m_scr[hd] = m if c == 0 else jnp.maximum(m_scr[hd], m)

    for c in range(nb // CHUNK_BLOCKS):
        @pl.when((c + 1) * CHUNK_BLOCKS <= i)
        def _():
            for hd in range(2):
                chunk_scores(c, hd, False)

        @pl.when((c * CHUNK_BLOCKS <= i) & (i < (c + 1) * CHUNK_BLOCKS))
        def _():
            for hd in range(2):
                chunk_scores(c, hd, True)

    for c in range(nb // CHUNK_BLOCKS):
        @pl.when(c * CHUNK_BLOCKS <= i)
        def _():
            for hd in range(2):
                p = jnp.exp2(s_scr[hd, c] - m_scr[hd]).astype(BF16)
                pv = _dot(vaug_scr[hd, :, c * ck:(c + 1) * ck], p)
                acc_scr[hd] = pv if c == 0 else acc_scr[hd] + pv

    o_t = jnp.concatenate([acc_scr[0, 0:M_DH, :] / acc_scr[0, M_DH:M_DH + 1, :],
                           acc_scr[1, 0:M_DH, :] / acc_scr[1, M_DH:M_DH + 1, :]], axis=0)
    o_ref[...] = o_t.T.astype(o_ref.dtype)


def _moba_prompt(mqt, k_all, v_all, layer):
    B, _, S = mqt.shape
    nb = S // M_BLOCK
    tq = M_BLOCK
    npair = M_W // LANES
    assert nb % CHUNK_BLOCKS == 0 and nb <= M_DH and nb % BF16_ROWS == 0
    return pl.pallas_call(
        _moba_prompt_kernel,
        grid=(B, npair, nb),
        in_specs=[pl.BlockSpec((None, LANES, tq), lambda b, p, i: (b, p, i)),
                  pl.BlockSpec((None, None, LANES, S), lambda b, p, i: (layer, b, p, 0)),
                  pl.BlockSpec((None, None, LANES, S), lambda b, p, i: (layer, b, p, 0))],
        out_specs=pl.BlockSpec((tq, LANES), lambda b, p, i: (b * nb + i, p)),
        out_shape=jax.ShapeDtypeStruct((B * S, M_W), BF16),
        scratch_shapes=[pltpu.VMEM((2, S, LANES), BF16),
                        pltpu.VMEM((2, V_ROWS, S), BF16),
                        pltpu.VMEM((nb, LANES), F32),
                        pltpu.VMEM((2, nb // CHUNK_BLOCKS, CHUNK_BLOCKS * M_BLOCK, tq), F32),
                        pltpu.VMEM((2, 1, tq), F32),
                        pltpu.VMEM((2, V_ROWS, tq), F32)],
        compiler_params=_params(3),
        name="moba_prompt",
    )(mqt, k_all, v_all)


PAGES_PER_STEP = 64


def _own_head(shape):
    row = lax.broadcasted_iota(jnp.int32, shape, 0)
    lane = lax.broadcasted_iota(jnp.int32, shape, 1)
    return (lane // M_DH) == (row % M_HEADS)


def _keys_sweep(step, q_ref, k_refs, s_ref, bsum_ref):
    @pl.when(step == 0)
    def _():
        bsum_ref[...] = jnp.zeros_like(bsum_ref)

    q = q_ref[...]
    qb = (jnp.where(_own_head(q.shape), q, 0.0) * (M_DH ** -0.5)).astype(BF16)
    ppb = M_BLOCK // PAGE_SIZE
    blk_lane = lax.broadcasted_iota(jnp.int32, bsum_ref.shape, 1)
    bsum = bsum_ref[...]
    for j, k_ref in enumerate(k_refs):
        kpt = k_ref[...].reshape(M_W, PAGE_SIZE)
        s_ref[:, j * PAGE_SIZE:(j + 1) * PAGE_SIZE] = _dot(qb, kpt.astype(BF16))
        blk_acc = kpt if j % ppb == 0 else blk_acc + kpt
        if j % ppb == ppb - 1:
            rowsum = jnp.sum(blk_acc, axis=1, keepdims=True)
            n = step * (len(k_refs) // ppb) + j // ppb
            bsum = bsum + jnp.where(blk_lane == n, rowsum, 0.0)
    bsum_ref[...] = bsum


def _page_specs(layer):
    def spec(j):
        return pl.BlockSpec(
            (None, None, M_HEADS, M_DH, PAGE_SIZE),
            lambda b, s, pt: (layer, pt[b, s * PAGES_PER_STEP + j], 0, 0, 0))
    return [spec(j) for j in range(PAGES_PER_STEP)]


def _sample_probs(s_ref, bsum_ref, q_ref, kn_ref, vn_ref, p_ref):
    q = q_ref[...]
    R = q.shape[0]
    qh = jnp.where(_own_head(q.shape), q, 0.0)
    means_t = bsum_ref[...] * (1.0 / M_BLOCK)
    nb = means_t.shape[1]
    sc = _dot(qh, means_t, precision=HIGHEST)
    n_idx = lax.broadcasted_iota(jnp.int32, (R, nb), 1)
    sel = _rank_select(sc, n_idx, nb, axis=1)
    s = s_ref[...]
    past = s.shape[1]
    blk_of_key = lax.broadcasted_iota(jnp.int32, (nb, past), 1) // M_BLOCK
    expand = jnp.where(blk_of_key == lax.broadcasted_iota(jnp.int32, (nb, past), 0), 1.0, 0.0)
    keep = _dot(sel.astype(BF16), expand.astype(BF16)) > 0.5
    s = jnp.where(keep, s, NEG)
    qb = (qh * (M_DH ** -0.5)).astype(BF16)
    s_own = _dot_nt(qb, kn_ref[...].astype(BF16))
    tok = lax.broadcasted_iota(jnp.int32, s_own.shape, 0) // M_HEADS
    s_own = jnp.where(lax.broadcasted_iota(jnp.int32, s_own.shape, 1) <= tok, s_own, NEG)
    m = jnp.maximum(jnp.max(s, axis=1, keepdims=True), jnp.max(s_own, axis=1, keepdims=True))
    p = jnp.exp(s - m)
    p_own = jnp.exp(s_own - m)
    inv = 1.0 / (jnp.sum(p, axis=1, keepdims=True) + jnp.sum(p_own, axis=1, keepdims=True))
    p_ref[...] = (p * inv).astype(p_ref.dtype)
    return _dot((p_own * inv).astype(BF16), vn_ref[...].astype(BF16))


def _moba_values_kernel(pt_ref, s_ref, bsum_ref, q_ref, kn_ref, vn_ref, *refs):
    v_refs = refs[:PAGES_PER_STEP]
    o_ref, p_scr = refs[PAGES_PER_STEP:]
    acc = _sample_probs(s_ref, bsum_ref, q_ref, kn_ref, vn_ref, p_scr)
    for j in range(PAGES_PER_STEP):
        vpt = v_refs[j][...].reshape(M_W, PAGE_SIZE).astype(BF16)
        acc = acc + _dot_nt(p_scr[:, j * PAGE_SIZE:(j + 1) * PAGE_SIZE], vpt)
    R = acc.shape[0]
    picked = jnp.where(_own_head(acc.shape), acc, 0.0)
    o_ref[...] = jnp.sum(picked.reshape(R // M_HEADS, M_HEADS, M_W), axis=1).astype(o_ref.dtype)


def _moba_values(page_table, scores, bsum_t, q_rep, k_new, v_new, cache_vt, layer):
    DB, n_pages = page_table.shape
    _, R, past = scores.shape
    nb = bsum_t.shape[2]
    LP = k_new.shape[1]
    assert n_pages == PAGES_PER_STEP
    per_b = lambda *shape: pl.BlockSpec((None,) + shape, lambda b, s, pt: (b, 0, 0))
    grid_spec = pltpu.PrefetchScalarGridSpec(
        num_scalar_prefetch=1,
        grid=(DB, 1),
        in_specs=[per_b(R, past), per_b(M_W, nb), per_b(R, M_W), per_b(LP, M_W), per_b(LP, M_W)]
        + _page_specs(layer),
        out_specs=per_b(R // M_HEADS, M_W),
        scratch_shapes=[pltpu.VMEM((R, past), BF16)],
    )
    return pl.pallas_call(
        _moba_values_kernel,
        grid_spec=grid_spec,
        out_shape=jax.ShapeDtypeStruct((DB, R // M_HEADS, M_W), F32),
        compiler_params=_params(2),
        name="moba_sample_values",
    )(page_table, scores, bsum_t, q_rep, k_new, v_new, *([cache_vt] * PAGES_PER_STEP))


def _merge_kernel(x_ref, ro_ref, mo_ref, gr_ref, ga_ref, wr_ref, wa_ref, wo_ref, y_ref):
    merged = (jax.nn.sigmoid(gr_ref[...].astype(F32)) * _dot(ro_ref[...].astype(BF16), wr_ref[...])
              + jax.nn.sigmoid(ga_ref[...].astype(F32)) * _dot(mo_ref[...].astype(BF16), wa_ref[...]))
    y_ref[...] = x_ref[...] + _dot(merged.astype(BF16), wo_ref[...])


def _merge(x, ro, mo, gr, ga, wr, wa, wo, layer, tm):
    T = x.shape[0]
    row = lambda i: (i, 0)
    blk = lambda w: pl.BlockSpec((tm, w), row)
    return pl.pallas_call(
        _merge_kernel,
        grid=(T // tm,),
        in_specs=[blk(D_MODEL), blk(R_V), blk(M_W), blk(D_MODEL), blk(D_MODEL),
                  _layer_resident(wr, layer), _layer_resident(wa, layer),
                  _layer_resident(wo, layer)],
        out_specs=blk(D_MODEL),
        out_shape=jax.ShapeDtypeStruct((T, D_MODEL), F32),
        compiler_params=_params(1),
        name="merge",
    )(x, ro, mo, gr, ga, wr, wa, wo)


FF_CHUNK = 256


def _ffn_keys_kernel(pt_ref, x_ref, g_ref, wgu_ref, wd_ref, gf_ref, q_ref, *refs,
                     final_norm, n_pages, sweeps_per_seq):
    k_refs = refs[:n_pages]
    y_ref, s_ref, bsum_ref, h_scr, act_scr = refs[n_pages:]
    _keys_sweep(pl.program_id(0) % sweeps_per_seq, q_ref, k_refs, s_ref, bsum_ref)
    _ffn_kernel(x_ref, g_ref, wgu_ref, wd_ref, gf_ref, y_ref, h_scr, act_scr,
                final_norm=final_norm)


def _ffn_kernel(x_ref, g_ref, wgu_ref, wd_ref, gf_ref, y_ref, h_scr, act_scr, *, final_norm):
    x = x_ref[...]
    h_scr[...] = _rms(x, g_ref[...]).astype(BF16)
    for c0 in range(0, D_FF, FF_CHUNK):
        gate = _dot(h_scr[...], wgu_ref[:, c0:c0 + FF_CHUNK])
        up = _dot(h_scr[...], wgu_ref[:, D_FF + c0:D_FF + c0 + FF_CHUNK])
        act_scr[:, c0:c0 + FF_CHUNK] = (gate * jax.nn.sigmoid(gate) * up).astype(BF16)
    y = x + _dot(act_scr[...], wd_ref[...])
    if final_norm:
        y = _rms(y, gf_ref[...])
    y_ref[...] = y


def _ffn(x, g, wgu, wd, g_final, layer, tm, final_norm):
    T = x.shape[0]
    row = lambda i: (i, 0)
    return pl.pallas_call(
        functools.partial(_ffn_kernel, final_norm=final_norm),
        grid=(T // tm,),
        in_specs=[pl.BlockSpec((tm, D_MODEL), row), _layer_resident(g, layer),
                  _layer_resident(wgu, layer), _layer_resident(wd, layer),
                  _resident((1, D_MODEL))],
        out_specs=pl.BlockSpec((tm, D_MODEL), row),
        out_shape=jax.ShapeDtypeStruct((T, D_MODEL), F32),
        scratch_shapes=[pltpu.VMEM((tm, D_MODEL), BF16), pltpu.VMEM((tm, D_FF), BF16)],
        compiler_params=_params(1),
        name="ffn",
    )(x, g, wgu, wd, g_final)


FFN_KEYS_TM = 256


def _ffn_keys(x, g, wgu, wd, g_final, layer, final_norm, page_table, q_rep, cache_kt):
    T = x.shape[0]
    tm = FFN_KEYS_TM
    steps = T // tm
    DB, n_pages = page_table.shape
    R = q_rep.shape[1]
    sweeps = steps // DB
    assert steps == DB * sweeps and n_pages % sweeps == 0
    pages = n_pages // sweeps
    nb = n_pages * PAGE_SIZE // M_BLOCK
    row = lambda i, pt: (i, 0)
    seq = lambda i, pt: (i // sweeps, 0, 0)
    lres = lambda a: pl.BlockSpec((None,) + a.shape[1:], lambda i, pt: (layer,) + (0,) * (a.ndim - 1),
                                  pipeline_mode=pl.Buffered(1))

    def page_spec(j):
        return pl.BlockSpec(
            (None, None, M_HEADS, M_DH, PAGE_SIZE),
            lambda i, pt: (layer, pt[i // sweeps, (i % sweeps) * pages + j], 0, 0, 0))

    grid_spec = pltpu.PrefetchScalarGridSpec(
        num_scalar_prefetch=1,
        grid=(steps,),
        in_specs=[pl.BlockSpec((tm, D_MODEL), row), lres(g), lres(wgu), lres(wd),
                  pl.BlockSpec((1, D_MODEL), lambda i, pt: (0, 0), pipeline_mode=pl.Buffered(1)),
                  pl.BlockSpec((None, R, M_W), seq)] + [page_spec(j) for j in range(pages)],
        out_specs=[pl.BlockSpec((tm, D_MODEL), row),
                   pl.BlockSpec((None, R, pages * PAGE_SIZE), lambda i, pt: (i // sweeps, 0, i % sweeps)),
                   pl.BlockSpec((None, M_W, nb), seq)],
        scratch_shapes=[pltpu.VMEM((tm, D_MODEL), BF16), pltpu.VMEM((tm, D_FF), BF16)],
    )
    return pl.pallas_call(
        functools.partial(_ffn_keys_kernel, final_norm=final_norm, n_pages=pages,
                          sweeps_per_seq=sweeps),
        grid_spec=grid_spec,
        out_shape=[jax.ShapeDtypeStruct((T, D_MODEL), F32),
                   jax.ShapeDtypeStruct((DB, R, n_pages * PAGE_SIZE), F32),
                   jax.ShapeDtypeStruct((DB, M_W, nb), F32)],
        compiler_params=_params(1),
        name="ffn_keys",
    )(page_table, x, g, wgu, wd, g_final, q_rep, *([cache_kt] * pages))


def _rotation_tables(pos):
    posf = pos.astype(F32)
    angle = 1.0 / (10000.0 ** jnp.linspace(0.0, 1.0, R_DK // 2, dtype=F32))
    ang = jnp.repeat(posf[:, None] * angle[None, :], 2, axis=-1)
    sign = jnp.where(jnp.arange(R_DK) % 2 == 0, -1.0, 1.0).astype(F32)
    half = M_DH // 2
    inv = ROPE_THETA ** (-jnp.arange(half, dtype=F32) / half)
    ang_m = (posf[:, None] * inv[None, :]).T
    return jnp.cos(ang), jnp.sin(ang) * sign[None, :], jnp.cos(ang_m), jnp.sin(ang_m)


def _feature_major_rows(t, n_seq):
    depth, _, _, S = t.shape
    return t.reshape(depth, n_seq, M_HEADS, M_DH, S).transpose(0, 1, 4, 2, 3)


def kernel(x_prompt, x_sample, cache_k, cache_v, state_ret, page_table, g_mix, w_in, w_br_ret,
           w_br_att, w_out, g_ffn, w_gu, w_down, g_final):
    depth = w_in.shape[0]
    g_mix = g_mix.reshape(depth, 1, D_MODEL)
    g_ffn = g_ffn.reshape(depth, 1, D_MODEL)
    g_final = g_final.reshape(1, D_MODEL)
    w_in, w_gu, w_down = w_in.astype(BF16), w_gu.astype(BF16), w_down.astype(BF16)
    w_br_ret, w_br_att, w_out = w_br_ret.astype(BF16), w_br_att.astype(BF16), w_out.astype(BF16)

    B, S, _ = x_prompt.shape
    DB, L, _ = x_sample.shape
    Ts = DB * L
    LP = 8
    tm = 512
    past_len = page_table.shape[1] * PAGE_SIZE
    xp = x_prompt.reshape(B * S, D_MODEL)
    xs = x_sample.reshape(Ts, D_MODEL)
    rot_p = _rotation_tables(jnp.arange(S))
    rot_s = _rotation_tables(past_len + (jnp.arange(Ts) % L))
    ret_tabs_p = _retention_tables(R_CHUNK, float(R_CHUNK))
    ret_tabs_s = _retention_tables(LP, float(L))
    ckt = cache_k.transpose(0, 1, 3, 4, 2)
    cvt = cache_v.transpose(0, 1, 3, 4, 2)

    def per_seq(t):
        return jnp.pad(t.reshape(DB, L, -1), ((0, 0), (0, LP - L), (0, 0)))

    kv_p, kv_s, states_p, states_s = None, None, [], None
    for l in range(depth):
        last = l == depth - 1
        rq, rk, rv, rg, gr_s, ga_s, mqt, *kv_s = _in_proj(xs, g_mix, w_in, rot_s, 1, Ts, F32, l, kv_s)
        mq, mk, mv = mqt[0].T, kv_s[0][l, 0].T, kv_s[1][l, 0].T
        ro_s, states_s = _retention_sample(per_seq(rq), per_seq(rk), per_seq(rv), per_seq(rg),
                                           state_ret, l, ret_tabs_s, states_s)
        q_rep = jnp.repeat(mq.reshape(DB, L, M_W), M_HEADS, axis=1)
        rq, rk, rv, rg, gr, ga, mqt, *kv_p = _in_proj(xp, g_mix, w_in, rot_p, B, tm, BF16, l, kv_p)
        ro, st = _retention_prompt(rq, rk, rv, rg, ret_tabs_p, B, S)
        states_p.append(st)
        mo = _moba_prompt(mqt, kv_p[0], kv_p[1], l)
        xp = _merge(xp, ro, mo, gr, ga, w_br_ret, w_br_att, w_out, l, tm)
        xp, scores, bsum_t = _ffn_keys(xp, g_ffn, w_gu, w_down, g_final, l, last,
                                       page_table, q_rep, ckt)
        mo_s = _moba_values(page_table, scores, bsum_t, q_rep, per_seq(mk), per_seq(mv), cvt, l)
        xs = _merge(xs, ro_s[:, :L].reshape(Ts, R_V), mo_s.reshape(Ts, M_W), gr_s, ga_s,
                    w_br_ret, w_br_att, w_out, l, Ts)
        xs = _ffn(xs, g_ffn, w_gu, w_down, g_final, l, Ts, last)

    def new_rows(t):
        return t[:, 0].transpose(0, 2, 1).reshape(depth, DB, L, M_HEADS, M_DH)

    return (xp.reshape(B, S, D_MODEL), xs.reshape(DB, L, D_MODEL),
            _feature_major_rows(kv_p[0], B), _feature_major_rows(kv_p[1], B),
            jnp.stack(states_p).astype(state_ret.dtype),
            new_rows(kv_s[0]), new_rows(kv_s[1]), states_s.astype(state_ret.dtype))
```

```python
import functools

import jax
import jax.numpy as jnp
from jax import lax
from jax.experimental import pallas as pl
from jax.experimental.pallas import tpu as pltpu

F32 = jnp.float32
BF16 = jnp.bfloat16

D_MODEL = 1024
R_HEADS = 4
R_DK = 128
R_DV = 256
R_CHUNK = 128
R_QK = R_HEADS * R_DK
R_V = R_HEADS * R_DV
M_HEADS = 8
M_DH = 64
M_W = M_HEADS * M_DH
M_BLOCK = 256
M_TOPK = 3
PAGE_SIZE = 128
ROPE_THETA = 10000.0
D_FF = 2816
NORM_EPS = 1e-6
NEG = -1e30

C_RQ, C_RK, C_RV, C_RG = 0, 512, 1024, 2048
C_MQ = 3072
C_GR, C_GA = 4608, 5632
IN_WIDTH = 6656

LANES = 128
VMEM_LIMIT = 56 * 1024 * 1024
HIGHEST = lax.Precision.HIGHEST

NT_DIMS = (((1,), (1,)), ((), ()))
TN_DIMS = (((0,), (0,)), ((), ()))


def _params(n_axes):
    return pltpu.CompilerParams(
        dimension_semantics=("arbitrary",) * n_axes, vmem_limit_bytes=VMEM_LIMIT)


def _dot(a, b, precision=None):
    return jnp.dot(a, b, precision=precision, preferred_element_type=F32)


def _dot_nt(a, b):
    return lax.dot_general(a, b, NT_DIMS, preferred_element_type=F32)


def _dot_tn(a, b):
    return lax.dot_general(a, b, TN_DIMS, preferred_element_type=F32)


def _resident(shape):
    nd = len(shape)
    return pl.BlockSpec(shape, lambda *_: (0,) * nd, pipeline_mode=pl.Buffered(1))


def _layer_resident(stacked, layer):
    nd = stacked.ndim - 1
    return pl.BlockSpec((None,) + stacked.shape[1:], lambda *_: (layer,) + (0,) * nd,
                        pipeline_mode=pl.Buffered(1))


def _rms(x, g):
    return x * lax.rsqrt(jnp.mean(x * x, axis=-1, keepdims=True) + NORM_EPS) * g


def _in_proj_kernel(x_ref, g_ref, w_ref, cr_ref, sr_ref, cmt_ref, smt_ref, *refs, first_layer):
    if first_layer:
        rq_ref, rk_ref, rv_ref, rg_ref, gr_ref, ga_ref, mqt_ref, k_all, v_all, h_scr = refs
        mkt_ref, mvt_ref = k_all.at[0], v_all.at[0]
        for later in range(1, k_all.shape[0]):
            k_all[later] = jnp.zeros(k_all.shape[1:], k_all.dtype)
            v_all[later] = jnp.zeros(v_all.shape[1:], v_all.dtype)
    else:
        _, _, rq_ref, rk_ref, rv_ref, rg_ref, gr_ref, ga_ref, mqt_ref, mkt_ref, mvt_ref, h_scr = refs
    h_scr[...] = _rms(x_ref[...], g_ref[...]).astype(BF16)

    def seg(c0, width=512):
        return _dot(h_scr[...], w_ref[:, c0:c0 + width])

    tm = h_scr.shape[0]
    even_lane = (lax.broadcasted_iota(jnp.int32, (tm, LANES), 1) % 2) == 0
    cr, sr = cr_ref[...], sr_ref[...]

    for out_ref, c0, scale in ((rq_ref, C_RQ, None), (rk_ref, C_RK, R_DK ** -0.5)):
        a = seg(c0)
        for hh in range(R_HEADS):
            sl = slice(hh * LANES, (hh + 1) * LANES)
            ah = a[:, sl]
            partner = jnp.where(even_lane, pltpu.roll(ah, LANES - 1, 1), pltpu.roll(ah, 1, 1))
            r = ah * cr + partner * sr
            if scale is not None:
                r = r * scale
            out_ref[:, sl] = r.astype(out_ref.dtype)
    for j in range(2):
        cols = slice(j * 512, (j + 1) * 512)
        rv_ref[:, cols] = seg(C_RV + j * 512).astype(rv_ref.dtype)
        rg_ref[:, cols] = seg(C_RG + j * 512).astype(rg_ref.dtype)
        gr_ref[:, cols] = seg(C_GR + j * 512).astype(gr_ref.dtype)
        ga_ref[:, cols] = seg(C_GA + j * 512).astype(ga_ref.dtype)

    cmt, smt = cmt_ref[...], smt_ref[...]
    half = M_DH // 2
    for idx, out_ref in enumerate((mqt_ref, mkt_ref, mvt_ref)):
        at = seg(C_MQ + idx * M_W).T
        if idx == 2:
            out_ref[...] = at
            continue
        for hd in range(M_HEADS):
            x1 = at[hd * M_DH:hd * M_DH + half]
            x2 = at[hd * M_DH + half:(hd + 1) * M_DH]
            out_ref[hd * M_DH:hd * M_DH + half, :] = x1 * cmt - x2 * smt
            out_ref[hd * M_DH + half:(hd + 1) * M_DH, :] = x2 * cmt + x1 * smt


def _in_proj(x, g, w, tabs, n_seq, tm, r_dtype, layer, kv_all):
    depth = w.shape[0]
    T = x.shape[0]
    S = T // n_seq
    per_seq = S // tm
    cr, sr, cmt, smt = tabs
    row = lambda i: (i, 0)
    tab_spec = pl.BlockSpec((tm, LANES), lambda i: (i % per_seq, 0))
    tabt_spec = pl.BlockSpec((M_DH // 2, tm), lambda i: (0, i % per_seq))
    widths = (R_QK, R_QK, R_V, R_V, D_MODEL, D_MODEL)
    dtypes = (r_dtype, r_dtype, r_dtype, F32, BF16, BF16)
    featmajor = pl.BlockSpec((None, M_W, tm), lambda i: (i // per_seq, 0, i % per_seq))
    in_specs = [pl.BlockSpec((tm, D_MODEL), row), _layer_resident(g, layer),
                _layer_resident(w, layer), tab_spec, tab_spec, tabt_spec, tabt_spec]
    n_plain = len(widths) + 1
    if layer == 0:
        kv_spec = pl.BlockSpec((depth, None, M_W, tm), lambda i: (0, i // per_seq, 0, i % per_seq))
        extra, aliases = (), {}
    else:
        kv_spec = pl.BlockSpec((None, None, M_W, tm),
                               lambda i: (layer, i // per_seq, 0, i % per_seq))
        extra = tuple(kv_all)
        aliases = {len(in_specs): n_plain, len(in_specs) + 1: n_plain + 1}
        in_specs = in_specs + [pl.BlockSpec(memory_space=pl.ANY)] * 2
    return pl.pallas_call(
        functools.partial(_in_proj_kernel, first_layer=layer == 0),
        grid=(T // tm,),
        in_specs=in_specs,
        out_specs=[pl.BlockSpec((tm, wd), row) for wd in widths] + [featmajor, kv_spec, kv_spec],
        out_shape=[jax.ShapeDtypeStruct((T, wd), dt) for wd, dt in zip(widths, dtypes)]
        + [jax.ShapeDtypeStruct((n_seq, M_W, S), F32)]
        + [jax.ShapeDtypeStruct((depth, n_seq, M_W, S), F32)] * 2,
        scratch_shapes=[pltpu.VMEM((tm, D_MODEL), BF16)],
        input_output_aliases=aliases,
        compiler_params=_params(1),
        name="in_proj",
    )(x, g, w, cr, sr, cmt, smt, *extra)


def _head_rms_gate(o, gate):
    on = o * lax.rsqrt(jnp.mean(o * o, axis=-1, keepdims=True) + NORM_EPS)
    return on * (gate * jax.nn.sigmoid(gate))


def _retention_head(q, k, v, s_prev, dec, qd, kd, gl):
    qf, kf = q.astype(F32), k.astype(F32)
    vb = v.astype(BF16)
    inner = _dot_nt(q.astype(BF16), k.astype(BF16)) * dec
    o = _dot(inner.astype(BF16), vb) + _dot((qf * qd).astype(BF16), s_prev.astype(BF16))
    s_new = gl * s_prev + _dot_tn((kf * kd).astype(BF16), vb)
    return o, s_new


RET_CHUNKS_PER_STEP = 4


def _ret_prompt_kernel(q_ref, k_ref, v_ref, g_ref, dec_ref, qd_ref, kd_ref, gl_ref,
                       o_ref, st_ref, s_scr):
    c = pl.program_id(1)

    @pl.when(c == 0)
    def _():
        s_scr[...] = jnp.zeros_like(s_scr)

    for hh in range(R_HEADS):
        ks = slice(hh * R_DK, (hh + 1) * R_DK)
        vs = slice(hh * R_DV, (hh + 1) * R_DV)
        s = s_scr[hh]
        for j in range(RET_CHUNKS_PER_STEP):
            rows = slice(j * R_CHUNK, (j + 1) * R_CHUNK)
            o, s = _retention_head(q_ref[rows, ks], k_ref[rows, ks], v_ref[rows, vs], s,
                                   dec_ref[hh], qd_ref[hh], kd_ref[hh], gl_ref[hh])
            o_ref[rows, vs] = _head_rms_gate(o, g_ref[rows, vs]).astype(o_ref.dtype)
        s_scr[hh] = s

    @pl.when(c == pl.num_programs(1) - 1)
    def _():
        st_ref[...] = s_scr[...]


def _retention_prompt(rq, rk, rv, rg, tabs, B, S):
    rows = RET_CHUNKS_PER_STEP * R_CHUNK
    nc = S // rows
    row = lambda b, c: (b * nc + c, 0)
    dec, qd, kd, gl = tabs
    return pl.pallas_call(
        _ret_prompt_kernel,
        grid=(B, nc),
        in_specs=[pl.BlockSpec((rows, R_QK), row), pl.BlockSpec((rows, R_QK), row),
                  pl.BlockSpec((rows, R_V), row), pl.BlockSpec((rows, R_V), row),
                  _resident(dec.shape), _resident(qd.shape), _resident(kd.shape),
                  _resident(gl.shape)],
        out_specs=[pl.BlockSpec((rows, R_V), row),
                   pl.BlockSpec((None, R_HEADS, R_DK, R_DV), lambda b, c: (b, 0, 0, 0))],
        out_shape=[jax.ShapeDtypeStruct((B * S, R_V), BF16),
                   jax.ShapeDtypeStruct((B, R_HEADS, R_DK, R_DV), F32)],
        scratch_shapes=[pltpu.VMEM((R_HEADS, R_DK, R_DV), F32)],
        compiler_params=_params(2),
        name="retention_prompt",
    )(rq, rk, rv, rg, dec, qd, kd, gl)


def _ret_sample_kernel(q_ref, k_ref, v_ref, g_ref, s_ref, dec_ref, qd_ref, kd_ref, gl_ref,
                       *refs, first_layer):
    if first_layer:
        o_ref, st_all = refs
        st_ref = st_all.at[0]
        for later in range(1, st_all.shape[0]):
            st_all[later] = jnp.zeros(st_all.shape[1:], st_all.dtype)
    else:
        _, o_ref, st_ref = refs
    for hh in range(R_HEADS):
        ks = slice(hh * R_DK, (hh + 1) * R_DK)
        vs = slice(hh * R_DV, (hh + 1) * R_DV)
        o, s_new = _retention_head(q_ref[:, ks], k_ref[:, ks], v_ref[:, vs], s_ref[hh],
                                   dec_ref[hh], qd_ref[hh], kd_ref[hh], gl_ref[hh])
        st_ref[hh] = s_new
        o_ref[:, vs] = _head_rms_gate(o, g_ref[:, vs]).astype(o_ref.dtype)


def _retention_sample(rq, rk, rv, rg, state_all, layer, tabs, new_states):
    DB, LP = rq.shape[:2]
    depth = state_all.shape[0]
    dec, qd, kd, gl = tabs
    tok = lambda w: pl.BlockSpec((None, LP, w), lambda b: (b, 0, 0))
    st_in = pl.BlockSpec((None, None, R_HEADS, R_DK, R_DV), lambda b: (layer, b, 0, 0, 0))
    in_specs = [tok(R_QK), tok(R_QK), tok(R_V), tok(R_V), st_in,
                _resident(dec.shape), _resident(qd.shape), _resident(kd.shape),
                _resident(gl.shape)]
    if layer == 0:
        st_out = pl.BlockSpec((depth, None, R_HEADS, R_DK, R_DV), lambda b: (0, b, 0, 0, 0))
        extra, aliases = (), {}
    else:
        st_out = st_in
        extra, aliases = (new_states,), {len(in_specs): 1}
        in_specs = in_specs + [pl.BlockSpec(memory_space=pl.ANY)]
    return pl.pallas_call(
        functools.partial(_ret_sample_kernel, first_layer=layer == 0),
        grid=(DB,),
        in_specs=in_specs,
        out_specs=[tok(R_V), st_out],
        out_shape=[jax.ShapeDtypeStruct((DB, LP, R_V), F32),
                   jax.ShapeDtypeStruct(state_all.shape, F32)],
        input_output_aliases=aliases,
        compiler_params=_params(1),
        name="retention_sample",
    )(rq, rk, rv, rg, state_all, dec, qd, kd, gl, *extra)


def _retention_tables(L, l_real):
    log_gamma = jnp.log1p(-jnp.exp2(-5.0 - jnp.arange(R_HEADS, dtype=F32)))
    n = jnp.arange(L, dtype=F32)
    diff = n[:, None] - n[None, :]
    dec = jnp.where(diff >= 0, jnp.exp(log_gamma[:, None, None] * jnp.maximum(diff, 0.0)), 0.0)
    q_decay = jnp.exp(log_gamma[:, None] * (n[None, :] + 1.0))
    k_decay = jnp.exp(log_gamma[:, None] * (l_real - 1.0 - n[None, :]))
    qd = jnp.broadcast_to(q_decay[:, :, None], (R_HEADS, L, R_DK))
    kd = jnp.broadcast_to(k_decay[:, :, None], (R_HEADS, L, R_DK))
    gl = jnp.broadcast_to(jnp.exp(log_gamma * l_real)[:, None, None], (R_HEADS, 1, R_DV))
    return dec, qd, kd, gl


def _rank_select(scores, n_idx, n_cand, axis):
    nb = scores.shape[axis]
    cnt = jnp.zeros(scores.shape, F32)
    for m in range(nb):
        sm = scores[m:m + 1, :] if axis == 0 else scores[:, m:m + 1]
        beats = (sm > scores) | ((sm == scores) & (m < n_idx))
        cnt = cnt + jnp.where(beats, jnp.where(m < n_cand, 1.0, 0.0), 0.0)
    return jnp.where((n_idx < n_cand) & (cnt < float(M_TOPK)), 1.0, 0.0)


BF16_ROWS = 16
V_ROWS = M_DH + BF16_ROWS
LOG2E = 1.4426950408889634
CHUNK_BLOCKS = 4


def _moba_prompt_kernel(qt_ref, kt_ref, vt_ref, o_ref, kaug_scr, vaug_scr, means_scr,
                        s_scr, m_scr, acc_scr):
    i = pl.program_id(2)
    nb = means_scr.shape[0]
    tq = qt_ref.shape[1]
    ck = CHUNK_BLOCKS * M_BLOCK
    own_rows = (slice(0, M_DH), slice(M_DH, LANES))
    onehot_lane0 = (M_DH, 0)

    @pl.when(i == 0)
    def _():
        lane = lax.broadcasted_iota(jnp.int32, (M_BLOCK, LANES), 1)
        ones_tile = jnp.ones((BF16_ROWS, M_BLOCK), BF16)
        for n in range(nb):
            cols = slice(n * M_BLOCK, (n + 1) * M_BLOCK)
            kblk = kt_ref[:, cols].T
            means_scr[n:n + 1, :] = jnp.sum(kblk, axis=0, keepdims=True) * (1.0 / M_BLOCK)
            for hd in range(2):
                mine = (lane < M_DH) if hd == 0 else (lane >= M_DH)
                other = jnp.where(lane == onehot_lane0[hd] + n, 1.0, 0.0)
                kaug_scr[hd, cols, :] = jnp.where(mine, kblk, other).astype(BF16)
                vaug_scr[hd, 0:M_DH, cols] = vt_ref[own_rows[hd], cols].astype(BF16)
                vaug_scr[hd, M_DH:V_ROWS, cols] = ones_tile

    qt = qt_ref[...]
    row = lax.broadcasted_iota(jnp.int32, qt.shape, 0)
    n_idx = lax.broadcasted_iota(jnp.int32, (nb, tq), 0)
    means = means_scr[...]
    pad = jnp.zeros((M_DH - nb, tq), BF16)
    rhs = []
    for hd in range(2):
        mine = (row < M_DH) if hd == 0 else (row >= M_DH)
        qh = jnp.where(mine, qt, 0.0)
        sel = _rank_select(_dot(means, qh, precision=HIGHEST), n_idx, i, axis=0)
        bias = jnp.where((sel > 0.5) | (n_idx == i), 0.0, NEG).astype(BF16)
        q_own = (qt[own_rows[hd]] * (M_DH ** -0.5 * LOG2E)).astype(BF16)
        parts = [q_own, bias, pad]
        rhs.append(jnp.concatenate(parts if hd == 0 else parts[1:] + parts[:1], axis=0))

    def chunk_scores(c, hd, with_causal):
        s = _dot(kaug_scr[hd, c * ck:(c + 1) * ck, :], rhs[hd])
        if with_causal:
            key = lax.broadcasted_iota(jnp.int32, (ck, tq), 0)
            qry = lax.broadcasted_iota(jnp.int32, (1, tq), 1) + (i - c * CHUNK_BLOCKS) * M_BLOCK
            s = jnp.where(key <= qry, s, NEG)
        s_scr[hd, c] = s
        m = jnp.max(s, axis=0, keepdims=True)
        m_scr[hd] = m if c == 0 else jnp.maximum(m_scr[hd], m)

    for c in range(nb // CHUNK_BLOCKS):
        @pl.when((c + 1) * CHUNK_BLOCKS <= i)
        def _():
            for hd in range(2):
                chunk_scores(c, hd, False)

        @pl.when((c * CHUNK_BLOCKS <= i) & (i < (c + 1) * CHUNK_BLOCKS))
        def _():
            for hd in range(2):
                chunk_scores(c, hd, True)

    for c in range(nb // CHUNK_BLOCKS):
        @pl.when(c * CHUNK_BLOCKS <= i)
        def _():
            for hd in range(2):
                p = jnp.exp2(s_scr[hd, c] - m_scr[hd]).astype(BF16)
                pv = _dot(vaug_scr[hd, :, c * ck:(c + 1) * ck], p)
                acc_scr[hd] = pv if c == 0 else acc_scr[hd] + pv

    o_t = jnp.concatenate([acc_scr[0, 0:M_DH, :] / acc_scr[0, M_DH:M_DH + 1, :],
                           acc_scr[1, 0:M_DH, :] / acc_scr[1, M_DH:M_DH + 1, :]], axis=0)
    o_ref[...] = o_t.T.astype(o_ref.dtype)


def _moba_prompt(mqt, k_all, v_all, layer):
    B, _, S = mqt.shape
    nb = S // M_BLOCK
    tq = M_BLOCK
    npair = M_W // LANES
    assert nb % CHUNK_BLOCKS == 0 and nb <= M_DH and nb % BF16_ROWS == 0
    return pl.pallas_call(
        _moba_prompt_kernel,
        grid=(B, npair, nb),
        in_specs=[pl.BlockSpec((None, LANES, tq), lambda b, p, i: (b, p, i)),
                  pl.BlockSpec((None, None, LANES, S), lambda b, p, i: (layer, b, p, 0)),
                  pl.BlockSpec((None, None, LANES, S), lambda b, p, i: (layer, b, p, 0))],
        out_specs=pl.BlockSpec((tq, LANES), lambda b, p, i: (b * nb + i, p)),
        out_shape=jax.ShapeDtypeStruct((B * S, M_W), BF16),
        scratch_shapes=[pltpu.VMEM((2, S, LANES), BF16),
                        pltpu.VMEM((2, V_ROWS, S), BF16),
                        pltpu.VMEM((nb, LANES), F32),
                        pltpu.VMEM((2, nb // CHUNK_BLOCKS, CHUNK_BLOCKS * M_BLOCK, tq), F32),
                        pltpu.VMEM((2, 1, tq), F32),
                        pltpu.VMEM((2, V_ROWS, tq), F32)],
        compiler_params=_params(3),
        name="moba_prompt",
    )(mqt, k_all, v_all)


PAGES_PER_STEP = 64


def _own_head(shape):
    row = lax.broadcasted_iota(jnp.int32, shape, 0)
    lane = lax.broadcasted_iota(jnp.int32, shape, 1)
    return (lane // M_DH) == (row % M_HEADS)


def _keys_sweep(step, q_ref, k_refs, s_ref, bsum_ref):
    @pl.when(step == 0)
    def _():
        bsum_ref[...] = jnp.zeros_like(bsum_ref)

    q = q_ref[...]
    qb = (jnp.where(_own_head(q.shape), q, 0.0) * (M_DH ** -0.5)).astype(BF16)
    ppb = M_BLOCK // PAGE_SIZE
    blk_lane = lax.broadcasted_iota(jnp.int32, bsum_ref.shape, 1)
    bsum = bsum_ref[...]
    for j, k_ref in enumerate(k_refs):
        kpt = k_ref[...].reshape(M_W, PAGE_SIZE)
        s_ref[:, j * PAGE_SIZE:(j + 1) * PAGE_SIZE] = _dot(qb, kpt.astype(BF16))
        blk_acc = kpt if j % ppb == 0 else blk_acc + kpt
        if j % ppb == ppb - 1:
            rowsum = jnp.sum(blk_acc, axis=1, keepdims=True)
            n = step * (len(k_refs) // ppb) + j // ppb
            bsum = bsum + jnp.where(blk_lane == n, rowsum, 0.0)
    bsum_ref[...] = bsum


def _page_specs(layer):
    def spec(j):
        return pl.BlockSpec(
            (None, None, M_HEADS, M_DH, PAGE_SIZE),
            lambda b, s, pt: (layer, pt[b, s * PAGES_PER_STEP + j], 0, 0, 0))
    return [spec(j) for j in range(PAGES_PER_STEP)]


def _sample_probs(s_ref, bsum_ref, q_ref, kn_ref, vn_ref, p_ref):
    q = q_ref[...]
    R = q.shape[0]
    qh = jnp.where(_own_head(q.shape), q, 0.0)
    means_t = bsum_ref[...] * (1.0 / M_BLOCK)
    nb = means_t.shape[1]
    sc = _dot(qh, means_t, precision=HIGHEST)
    n_idx = lax.broadcasted_iota(jnp.int32, (R, nb), 1)
    sel = _rank_select(sc, n_idx, nb, axis=1)
    s = s_ref[...]
    past = s.shape[1]
    blk_of_key = lax.broadcasted_iota(jnp.int32, (nb, past), 1) // M_BLOCK
    expand = jnp.where(blk_of_key == lax.broadcasted_iota(jnp.int32, (nb, past), 0), 1.0, 0.0)
    keep = _dot(sel.astype(BF16), expand.astype(BF16)) > 0.5
    s = jnp.where(keep, s, NEG)
    qb = (qh * (M_DH ** -0.5)).astype(BF16)
    s_own = _dot_nt(qb, kn_ref[...].astype(BF16))
    tok = lax.broadcasted_iota(jnp.int32, s_own.shape, 0) // M_HEADS
    s_own = jnp.where(lax.broadcasted_iota(jnp.int32, s_own.shape, 1) <= tok, s_own, NEG)
    m = jnp.maximum(jnp.max(s, axis=1, keepdims=True), jnp.max(s_own, axis=1, keepdims=True))
    p = jnp.exp(s - m)
    p_own = jnp.exp(s_own - m)
    inv = 1.0 / (jnp.sum(p, axis=1, keepdims=True) + jnp.sum(p_own, axis=1, keepdims=True))
    p_ref[...] = (p * inv).astype(p_ref.dtype)
    return _dot((p_own * inv).astype(BF16), vn_ref[...].astype(BF16))


def _moba_values_kernel(pt_ref, s_ref, bsum_ref, q_ref, kn_ref, vn_ref, *refs):
    v_refs = refs[:PAGES_PER_STEP]
    o_ref, p_scr = refs[PAGES_PER_STEP:]
    acc = _sample_probs(s_ref, bsum_ref, q_ref, kn_ref, vn_ref, p_scr)
    for j in range(PAGES_PER_STEP):
        vpt = v_refs[j][...].reshape(M_W, PAGE_SIZE).astype(BF16)
        acc = acc + _dot_nt(p_scr[:, j * PAGE_SIZE:(j + 1) * PAGE_SIZE], vpt)
    R = acc.shape[0]
    picked = jnp.where(_own_head(acc.shape), acc, 0.0)
    o_ref[...] = jnp.sum(picked.reshape(R // M_HEADS, M_HEADS, M_W), axis=1).astype(o_ref.dtype)


def _moba_values(page_table, scores, bsum_t, q_rep, k_new, v_new, cache_vt, layer):
    DB, n_pages = page_table.shape
    _, R, past = scores.shape
    nb = bsum_t.shape[2]
    LP = k_new.shape[1]
    assert n_pages == PAGES_PER_STEP
    per_b = lambda *shape: pl.BlockSpec((None,) + shape, lambda b, s, pt: (b, 0, 0))
    grid_spec = pltpu.PrefetchScalarGridSpec(
        num_scalar_prefetch=1,
        grid=(DB, 1),
        in_specs=[per_b(R, past), per_b(M_W, nb), per_b(R, M_W), per_b(LP, M_W), per_b(LP, M_W)]
        + _page_specs(layer),
        out_specs=per_b(R // M_HEADS, M_W),
        scratch_shapes=[pltpu.VMEM((R, past), BF16)],
    )
    return pl.pallas_call(
        _moba_values_kernel,
        grid_spec=grid_spec,
        out_shape=jax.ShapeDtypeStruct((DB, R // M_HEADS, M_W), F32),
        compiler_params=_params(2),
        name="moba_sample_values",
    )(page_table, scores, bsum_t, q_rep, k_new, v_new, *([cache_vt] * PAGES_PER_STEP))


def _merge_kernel(x_ref, ro_ref, mo_ref, gr_ref, ga_ref, wr_ref, wa_ref, wo_ref, y_ref):
    merged = (jax.nn.sigmoid(gr_ref[...].astype(F32)) * _dot(ro_ref[...].astype(BF16), wr_ref[...])
              + jax.nn.sigmoid(ga_ref[...].astype(F32)) * _dot(mo_ref[...].astype(BF16), wa_ref[...]))
    y_ref[...] = x_ref[...] + _dot(merged.astype(BF16), wo_ref[...])


MERGE_KEYS_TM = 256


def _merge_keys_kernel(pt_ref, x_ref, ro_ref, mo_ref, gr_ref, ga_ref, wr_ref, wa_ref, wo_ref,
                       q_ref, *refs, n_pages, sweeps_per_seq):
    k_refs = refs[:n_pages]
    y_ref, s_ref, bsum_ref = refs[n_pages:]
    _keys_sweep(pl.program_id(0) % sweeps_per_seq, q_ref, k_refs, s_ref, bsum_ref)
    _merge_kernel(x_ref, ro_ref, mo_ref, gr_ref, ga_ref, wr_ref, wa_ref, wo_ref, y_ref)


def _merge_keys(x, ro, mo, gr, ga, wr, wa, wo, layer, tm, page_table, q_rep, cache_kt):
    T = x.shape[0]
    steps = T // tm
    DB, n_pages = page_table.shape
    R = q_rep.shape[1]
    sweeps = steps // DB
    assert steps == DB * sweeps and n_pages % sweeps == 0
    pages = n_pages // sweeps
    nb = n_pages * PAGE_SIZE // M_BLOCK
    row = lambda i, pt: (i, 0)
    blk = lambda w: pl.BlockSpec((tm, w), row)
    seq = lambda i, pt: (i // sweeps, 0, 0)
    lres = lambda a: pl.BlockSpec((None,) + a.shape[1:], lambda i, pt: (layer,) + (0,) * (a.ndim - 1),
                                  pipeline_mode=pl.Buffered(1))

    def page_spec(j):
        return pl.BlockSpec(
            (None, None, M_HEADS, M_DH, PAGE_SIZE),
            lambda i, pt: (layer, pt[i // sweeps, (i % sweeps) * pages + j], 0, 0, 0))

    grid_spec = pltpu.PrefetchScalarGridSpec(
        num_scalar_prefetch=1,
        grid=(steps,),
        in_specs=[blk(D_MODEL), blk(R_V), blk(M_W), blk(D_MODEL), blk(D_MODEL),
                  lres(wr), lres(wa), lres(wo), pl.BlockSpec((None, R, M_W), seq)]
        + [page_spec(j) for j in range(pages)],
        out_specs=[blk(D_MODEL),
                   pl.BlockSpec((None, R, pages * PAGE_SIZE), lambda i, pt: (i // sweeps, 0, i % sweeps)),
                   pl.BlockSpec((None, M_W, nb), seq)],
    )
    return pl.pallas_call(
        functools.partial(_merge_keys_kernel, n_pages=pages, sweeps_per_seq=sweeps),
        grid_spec=grid_spec,
        out_shape=[jax.ShapeDtypeStruct((T, D_MODEL), F32),
                   jax.ShapeDtypeStruct((DB, R, n_pages * PAGE_SIZE), F32),
                   jax.ShapeDtypeStruct((DB, M_W, nb), F32)],
        compiler_params=_params(1),
        name="merge_keys",
    )(page_table, x, ro, mo, gr, ga, wr, wa, wo, q_rep, *([cache_kt] * pages))


def _merge(x, ro, mo, gr, ga, wr, wa, wo, layer, tm):
    T = x.shape[0]
    row = lambda i: (i, 0)
    blk = lambda w: pl.BlockSpec((tm, w), row)
    return pl.pallas_call(
        _merge_kernel,
        grid=(T // tm,),
        in_specs=[blk(D_MODEL), blk(R_V), blk(M_W), blk(D_MODEL), blk(D_MODEL),
                  _layer_resident(wr, layer), _layer_resident(wa, layer),
                  _layer_resident(wo, layer)],
        out_specs=blk(D_MODEL),
        out_shape=jax.ShapeDtypeStruct((T, D_MODEL), F32),
        compiler_params=_params(1),
        name="merge",
    )(x, ro, mo, gr, ga, wr, wa, wo)


FF_CHUNK = 256


def _ffn_kernel(x_ref, g_ref, wgu_ref, wd_ref, gf_ref, y_ref, h_scr, act_scr, *, final_norm):
    x = x_ref[...]
    h_scr[...] = _rms(x, g_ref[...]).astype(BF16)
    for c0 in range(0, D_FF, FF_CHUNK):
        gate = _dot(h_scr[...], wgu_ref[:, c0:c0 + FF_CHUNK])
        up = _dot(h_scr[...], wgu_ref[:, D_FF + c0:D_FF + c0 + FF_CHUNK])
        act_scr[:, c0:c0 + FF_CHUNK] = (gate * jax.nn.sigmoid(gate) * up).astype(BF16)
    y = x + _dot(act_scr[...], wd_ref[...])
    if final_norm:
        y = _rms(y, gf_ref[...])
    y_ref[...] = y


def _ffn(x, g, wgu, wd, g_final, layer, tm, final_norm):
    T = x.shape[0]
    row = lambda i: (i, 0)
    return pl.pallas_call(
        functools.partial(_ffn_kernel, final_norm=final_norm),
        grid=(T // tm,),
        in_specs=[pl.BlockSpec((tm, D_MODEL), row), _layer_resident(g, layer),
                  _layer_resident(wgu, layer), _layer_resident(wd, layer),
                  _resident((1, D_MODEL))],
        out_specs=pl.BlockSpec((tm, D_MODEL), row),
        out_shape=jax.ShapeDtypeStruct((T, D_MODEL), F32),
        scratch_shapes=[pltpu.VMEM((tm, D_MODEL), BF16), pltpu.VMEM((tm, D_FF), BF16)],
        compiler_params=_params(1),
        name="ffn",
    )(x, g, wgu, wd, g_final)


def _rotation_tables(pos):
    posf = pos.astype(F32)
    angle = 1.0 / (10000.0 ** jnp.linspace(0.0, 1.0, R_DK // 2, dtype=F32))
    ang = jnp.repeat(posf[:, None] * angle[None, :], 2, axis=-1)
    sign = jnp.where(jnp.arange(R_DK) % 2 == 0, -1.0, 1.0).astype(F32)
    half = M_DH // 2
    inv = ROPE_THETA ** (-jnp.arange(half, dtype=F32) / half)
    ang_m = (posf[:, None] * inv[None, :]).T
    return jnp.cos(ang), jnp.sin(ang) * sign[None, :], jnp.cos(ang_m), jnp.sin(ang_m)


def _feature_major_rows(t, n_seq):
    depth, _, _, S = t.shape
    return t.reshape(depth, n_seq, M_HEADS, M_DH, S).transpose(0, 1, 4, 2, 3)


def kernel(x_prompt, x_sample, cache_k, cache_v, state_ret, page_table, g_mix, w_in, w_br_ret,
           w_br_att, w_out, g_ffn, w_gu, w_down, g_final):
    depth = w_in.shape[0]
    g_mix = g_mix.reshape(depth, 1, D_MODEL)
    g_ffn = g_ffn.reshape(depth, 1, D_MODEL)
    g_final = g_final.reshape(1, D_MODEL)
    w_in, w_gu, w_down = w_in.astype(BF16), w_gu.astype(BF16), w_down.astype(BF16)
    w_br_ret, w_br_att, w_out = w_br_ret.astype(BF16), w_br_att.astype(BF16), w_out.astype(BF16)

    B, S, _ = x_prompt.shape
    DB, L, _ = x_sample.shape
    Ts = DB * L
    LP = 8
    tm = 512
    past_len = page_table.shape[1] * PAGE_SIZE
    xp = x_prompt.reshape(B * S, D_MODEL)
    xs = x_sample.reshape(Ts, D_MODEL)
    rot_p = _rotation_tables(jnp.arange(S))
    rot_s = _rotation_tables(past_len + (jnp.arange(Ts) % L))
    ret_tabs_p = _retention_tables(R_CHUNK, float(R_CHUNK))
    ret_tabs_s = _retention_tables(LP, float(L))
    ckt = cache_k.transpose(0, 1, 3, 4, 2)
    cvt = cache_v.transpose(0, 1, 3, 4, 2)

    def per_seq(t):
        return jnp.pad(t.reshape(DB, L, -1), ((0, 0), (0, LP - L), (0, 0)))

    kv_p, kv_s, states_p, states_s = None, None, [], None
    for l in range(depth):
        last = l == depth - 1
        rq, rk, rv, rg, gr_s, ga_s, mqt, *kv_s = _in_proj(xs, g_mix, w_in, rot_s, 1, Ts, F32, l, kv_s)
        mq, mk, mv = mqt[0].T, kv_s[0][l, 0].T, kv_s[1][l, 0].T
        ro_s, states_s = _retention_sample(per_seq(rq), per_seq(rk), per_seq(rv), per_seq(rg),
                                           state_ret, l, ret_tabs_s, states_s)
        q_rep = jnp.repeat(mq.reshape(DB, L, M_W), M_HEADS, axis=1)
        rq, rk, rv, rg, gr, ga, mqt, *kv_p = _in_proj(xp, g_mix, w_in, rot_p, B, tm, BF16, l, kv_p)
        ro, st = _retention_prompt(rq, rk, rv, rg, ret_tabs_p, B, S)
        states_p.append(st)
        mo = _moba_prompt(mqt, kv_p[0], kv_p[1], l)
        xp, scores, bsum_t = _merge_keys(xp, ro, mo, gr, ga, w_br_ret, w_br_att, w_out, l,
                                         MERGE_KEYS_TM, page_table, q_rep, ckt)
        xp = _ffn(xp, g_ffn, w_gu, w_down, g_final, l, tm, last)
        mo_s = _moba_values(page_table, scores, bsum_t, q_rep, per_seq(mk), per_seq(mv), cvt, l)
        xs = _merge(xs, ro_s[:, :L].reshape(Ts, R_V), mo_s.reshape(Ts, M_W), gr_s, ga_s,
                    w_br_ret, w_br_att, w_out, l, Ts)
        xs = _ffn(xs, g_ffn, w_gu, w_down, g_final, l, Ts, last)

    def new_rows(t):
        return t[:, 0].transpose(0, 2, 1).reshape(depth, DB, L, M_HEADS, M_DH)

    return (xp.reshape(B, S, D_MODEL), xs.reshape(DB, L, D_MODEL),
            _feature_major_rows(kv_p[0], B), _feature_major_rows(kv_p[1], B),
            jnp.stack(states_p).astype(state_ret.dtype),
            new_rows(kv_s[0]), new_rows(kv_s[1]), states_s.astype(state_ret.dtype))
```
